```python
import jax, jax.numpy as jnp
from jax import lax
import numpy as np

D_MODEL = 1024
BATCH = 16
SEQ = 2048
DEPTH = 4
DEC_BATCH = 8
DEC_SEQ = 64
PAST_LEN = 2048

CHUNK = 64
N_MEM = 256
HG_DK = 128
HG_WIDTH = D_MODEL // 2
HG_HEADS = HG_WIDTH // HG_DK
CONV_WIDTH = D_MODEL // 4
CONV_K = 31
POOL_WIDTH = D_MODEL // 4
POOL_WINDOWS = (2, 4, 8, 16)
POOL_GROUPS = len(POOL_WINDOWS)
POOL_GDIM = POOL_WIDTH // POOL_GROUPS
POOL_HIST = max(POOL_WINDOWS) - 1
D_MIX = HG_WIDTH + CONV_WIDTH + POOL_WIDTH
D_IN = 4 * HG_WIDTH + 2 * CONV_WIDTH + POOL_WIDTH
IN_SPLITS = (HG_WIDTH, 2 * HG_WIDTH, 3 * HG_WIDTH, 4 * HG_WIDTH,
             4 * HG_WIDTH + CONV_WIDTH, 4 * HG_WIDTH + 2 * CONV_WIDTH)
X_HEADS = 4
X_HDIM = D_MODEL // X_HEADS
D_FF = 4 * D_MODEL
EPS = 1e-6
F_FLOOR = 1e-30

kernel_name = "hymba_streaming_hgrn2_conformer_pool"


def rms_norm(x, g):
    xf = x.astype(jnp.float32)
    y = xf * lax.rsqrt(jnp.mean(xf * xf, axis=-1, keepdims=True) + EPS)
    return (y * g.astype(jnp.float32)).astype(x.dtype)


def layer_norm(x, g, b):
    xf = x.astype(jnp.float32)
    xc = xf - jnp.mean(xf, axis=-1, keepdims=True)
    y = xc * lax.rsqrt(jnp.mean(xc * xc, axis=-1, keepdims=True) + EPS)
    return (y * g.astype(jnp.float32) + b.astype(jnp.float32)).astype(x.dtype)


def hgrn2_block(S0, q, k, v, log_f):
    T = q.shape[2]
    A = jnp.cumsum(log_f, axis=2)
    causal = jnp.tril(jnp.ones((T, T), dtype=bool))[:, :, None]
    diff = A[:, :, :, None, :] - A[:, :, None, :, :]
    decay = jnp.where(causal, jnp.exp(jnp.where(causal, diff, 0.0)), 0.0)
    scores = jnp.einsum("bhtd,bhsd,bhtsd->bhts", q, k, decay)
    o = (jnp.einsum("bhts,bhsv->bhtv", scores, v)
         + jnp.einsum("bhtd,bhdv->bhtv", q * jnp.exp(A), S0))
    A_end = A[:, :, -1:, :]
    S_new = (jnp.exp(A_end[:, :, 0, :])[..., None] * S0
             + jnp.einsum("bhsd,bhsv->bhdv", k * jnp.exp(A_end - A), v))
    return S_new, o


def hgrn2_recurrence(S0, q, k, v, log_f):
    B, H, T, _ = q.shape
    if T <= CHUNK:
        return hgrn2_block(S0, q, k, v, log_f)
    nc = T // CHUNK

    def to_blocks(t):
        return jnp.moveaxis(t.reshape(B, H, nc, CHUNK, t.shape[-1]), 2, 0)

    S, o = lax.scan(lambda s, xs: hgrn2_block(s, *xs), S0,
                    (to_blocks(q), to_blocks(k), to_blocks(v), to_blocks(log_f)))
    o = jnp.moveaxis(o, 0, 2).reshape(B, H, T, o.shape[-1])
    return S, o


def token_mixers(h, hg_state, conv_buf, pool_buf, pos0, lb, w_in, hg_norm_g,
                 conv_w, conv_b, conv_ln_g, conv_ln_b, pool_w, pool_scale):
    B, T, _ = h.shape
    f32 = jnp.float32
    z = h @ w_in
    q, fl, iv, og, ca, cg, pu = jnp.split(z, IN_SPLITS, axis=-1)

    def heads(t):
        return t.reshape(B, T, HG_HEADS, HG_DK).transpose(0, 2, 1, 3).astype(f32)
    qh = heads(jax.nn.silu(q))
    fh = heads(fl)
    vh = heads(iv)
    lbh = lb.astype(f32).reshape(HG_HEADS, 1, HG_DK)
    f_gate = lbh + (1.0 - lbh) * jax.nn.sigmoid(fh)
    log_f = jnp.log(jnp.maximum(f_gate, F_FLOOR))
    kh = (1.0 - lbh) * jax.nn.sigmoid(-fh)
    S_new, o = hgrn2_recurrence(hg_state.astype(f32), qh, kh, vh, log_f)
    o = o.transpose(0, 2, 1, 3)
    o = rms_norm(o, hg_norm_g.reshape(HG_HEADS, HG_DK)).reshape(B, T, HG_WIDTH)
    a_out = (o * jax.nn.silu(og.astype(f32))).astype(h.dtype)

    glu = ca * jax.nn.sigmoid(cg)
    full_c = jnp.concatenate([conv_buf.astype(glu.dtype), glu], axis=1)
    dw = lax.conv_general_dilated(
        full_c, conv_w[:, None, :].astype(full_c.dtype), window_strides=(1,),
        padding="VALID", dimension_numbers=("NWC", "WIO", "NWC"),
        feature_group_count=CONV_WIDTH) + conv_b.astype(full_c.dtype)
    b_out = jax.nn.silu(layer_norm(dw, conv_ln_g, conv_ln_b)).astype(h.dtype)
    new_conv_buf = full_c[:, -(CONV_K - 1):]

    full_p = jnp.concatenate([pool_buf.astype(pu.dtype), pu], axis=1)
    cs = jnp.cumsum(full_p.astype(f32), axis=1)
    cs = jnp.pad(cs, ((0, 0), (1, 0), (0, 0)))
    pos = pos0 + jnp.arange(T)
    diffs = []
    for gi, w in enumerate(POOL_WINDOWS):
        lo, hi = gi * POOL_GDIM, (gi + 1) * POOL_GDIM
        win_sum = cs[:, POOL_HIST + 1:, lo:hi] - cs[:, POOL_HIST + 1 - w:POOL_HIST + 1 - w + T, lo:hi]
        cnt = jnp.minimum(pos + 1, w).astype(f32)[None, :, None]
        diffs.append(win_sum / cnt - pu[:, :, lo:hi].astype(f32))
    dpool = jnp.stack(diffs, axis=2)
    c_out = (jnp.einsum("btgc,gcd->btgd", dpool, pool_w.astype(f32)).reshape(B, T, POOL_WIDTH)
             * pool_scale.astype(f32)).astype(h.dtype)
    new_pool_buf = full_p[:, -POOL_HIST:]

    mix = jnp.concatenate([a_out, b_out, c_out], axis=-1)
    return mix, S_new.astype(hg_state.dtype), new_conv_buf, new_pool_buf


def cross_attention(h, mk, mv, wq, wo):
    B, T, _ = h.shape
    q = (h @ wq).reshape(B, T, X_HEADS, X_HDIM)
    s = jnp.einsum("bthd,bmhd->bhtm", q, mk).astype(jnp.float32) / np.float32(np.sqrt(X_HDIM))
    p = jax.nn.softmax(s, axis=-1).astype(h.dtype)
    o = jnp.einsum("bhtm,bmhd->bthd", p, mv).reshape(B, T, D_MODEL)
    return o @ wo


def run_trunk(x, mem_k, mem_v, hg_state, conv_buf, pool_buf, pos0, lb_all, p):
    new_hg, new_conv, new_pool = [], [], []
    for li in range(DEPTH):
        h = rms_norm(x, p["norm_mix_g"][li])
        mix, s_hg, s_conv, s_pool = token_mixers(
            h, hg_state[li], conv_buf[li], pool_buf[li], pos0, lb_all[li],
            p["w_in"][li], p["hg_norm_g"][li], p["conv_w"][li], p["conv_b"][li],
            p["conv_ln_g"][li], p["conv_ln_b"][li], p["pool_w"][li], p["pool_scale"][li])
        x = x + mix @ p["w_out"][li]
        x = x + cross_attention(rms_norm(x, p["norm_x_g"][li]), mem_k[li], mem_v[li],
                                p["xq_w"][li], p["xo_w"][li])
        h = rms_norm(x, p["norm_ffn_g"][li])
        x = x + jnp.square(jax.nn.relu(h @ p["w_up"][li])) @ p["w_down"][li]
        new_hg.append(s_hg)
        new_conv.append(s_conv)
        new_pool.append(s_pool)
    return rms_norm(x, p["final_g"]), jnp.stack(new_hg), jnp.stack(new_conv), jnp.stack(new_pool)


def setup_inputs(seed: int = 0) -> dict:
    key = jax.random.key(seed)
    ks = jax.random.split(key, 32)
    f32 = jnp.float32

    def nrm(k, shape, scale):
        return jax.random.normal(k, shape, f32) * scale

    def gain(k, shape):
        return 1.0 + 0.05 * jax.random.normal(k, shape, f32)

    return {
        "x_prompt": nrm(ks[0], (BATCH, SEQ, D_MODEL), 1.0),
        "x_sample": nrm(ks[1], (DEC_BATCH, DEC_SEQ, D_MODEL), 1.0),
        "mem_prompt": nrm(ks[2], (BATCH, N_MEM, D_MODEL), 1.0),
        "state_hgrn": nrm(ks[3], (DEPTH, DEC_BATCH, HG_HEADS, HG_DK, HG_DK), 0.5),
        "cache_conv": nrm(ks[4], (DEPTH, DEC_BATCH, CONV_K - 1, CONV_WIDTH), 0.5),
        "cache_pool": nrm(ks[5], (DEPTH, DEC_BATCH, POOL_HIST, POOL_WIDTH), 1.0),
        "cache_mem_k": nrm(ks[6], (DEPTH, DEC_BATCH, N_MEM, X_HEADS, X_HDIM), 1.0),
        "cache_mem_v": nrm(ks[7], (DEPTH, DEC_BATCH, N_MEM, X_HEADS, X_HDIM), 1.0),
        "norm_mix_g": gain(ks[8], (DEPTH, D_MODEL)),
        "w_in": nrm(ks[9], (DEPTH, D_MODEL, D_IN), D_MODEL ** -0.5),
        "lb_param": nrm(ks[10], (DEPTH, HG_WIDTH), 0.1),
        "hg_norm_g": gain(ks[11], (DEPTH, HG_WIDTH)),
        "conv_w": nrm(ks[12], (DEPTH, CONV_K, CONV_WIDTH), CONV_K ** -0.5),
        "conv_b": nrm(ks[13], (DEPTH, CONV_WIDTH), 0.02),
        "conv_ln_g": gain(ks[14], (DEPTH, CONV_WIDTH)),
        "conv_ln_b": nrm(ks[15], (DEPTH, CONV_WIDTH), 0.02),
        "pool_w": nrm(ks[16], (DEPTH, POOL_GROUPS, POOL_GDIM, POOL_GDIM), POOL_GDIM ** -0.5),
        "pool_scale": gain(ks[17], (DEPTH, POOL_WIDTH)),
        "w_out": nrm(ks[18], (DEPTH, D_MIX, D_MODEL), D_MIX ** -0.5),
        "norm_x_g": gain(ks[19], (DEPTH, D_MODEL)),
        "norm_mem_g": gain(ks[20], (DEPTH, D_MODEL)),
        "xq_w": nrm(ks[21], (DEPTH, D_MODEL, D_MODEL), D_MODEL ** -0.5),
        "xk_w": nrm(ks[22], (DEPTH, D_MODEL, D_MODEL), D_MODEL ** -0.5),
        "xv_w": nrm(ks[23], (DEPTH, D_MODEL, D_MODEL), D_MODEL ** -0.5),
        "xo_w": nrm(ks[24], (DEPTH, D_MODEL, D_MODEL), D_MODEL ** -0.5),
        "norm_ffn_g": gain(ks[25], (DEPTH, D_MODEL)),
        "w_up": nrm(ks[26], (DEPTH, D_MODEL, D_FF), D_MODEL ** -0.5),
        "w_down": nrm(ks[27], (DEPTH, D_FF, D_MODEL), D_FF ** -0.5),
        "final_g": gain(ks[28], (D_MODEL,)),
    }


def reference(x_prompt, x_sample, mem_prompt, state_hgrn, cache_conv, cache_pool,
              cache_mem_k, cache_mem_v, norm_mix_g, w_in, lb_param, hg_norm_g,
              conv_w, conv_b, conv_ln_g, conv_ln_b, pool_w, pool_scale, w_out,
              norm_x_g, norm_mem_g, xq_w, xk_w, xv_w, xo_w, norm_ffn_g, w_up,
              w_down, final_g):
    p = dict(norm_mix_g=norm_mix_g, w_in=w_in, hg_norm_g=hg_norm_g, conv_w=conv_w,
             conv_b=conv_b, conv_ln_g=conv_ln_g, conv_ln_b=conv_ln_b, pool_w=pool_w,
             pool_scale=pool_scale, w_out=w_out, norm_x_g=norm_x_g, xq_w=xq_w, xo_w=xo_w,
             norm_ffn_g=norm_ffn_g, w_up=w_up, w_down=w_down, final_g=final_g)

    sm = jax.nn.softmax(lb_param.astype(jnp.float32), axis=0)
    lb_all = jnp.cumsum(sm, axis=0) - sm[0:1]

    Bp = mem_prompt.shape[0]
    mem_k_p = jnp.stack([
        (rms_norm(mem_prompt, norm_mem_g[li]) @ xk_w[li]).reshape(Bp, N_MEM, X_HEADS, X_HDIM)
        for li in range(DEPTH)])
    mem_v_p = jnp.stack([
        (rms_norm(mem_prompt, norm_mem_g[li]) @ xv_w[li]).reshape(Bp, N_MEM, X_HEADS, X_HDIM)
        for li in range(DEPTH)])

    dt = x_prompt.dtype
    hg0 = jnp.zeros((DEPTH, Bp, HG_HEADS, HG_DK, HG_DK), dt)
    conv0 = jnp.zeros((DEPTH, Bp, CONV_K - 1, CONV_WIDTH), dt)
    pool0 = jnp.zeros((DEPTH, Bp, POOL_HIST, POOL_WIDTH), dt)
    y_prompt, hg_p, conv_p, pool_p = run_trunk(
        x_prompt, mem_k_p, mem_v_p, hg0, conv0, pool0, 0, lb_all, p)

    y_sample, hg_s, conv_s, pool_s = run_trunk(
        x_sample, cache_mem_k, cache_mem_v, state_hgrn, cache_conv, cache_pool,
        PAST_LEN, lb_all, p)

    return (y_prompt, y_sample, hg_p, conv_p, pool_p, mem_k_p, mem_v_p, hg_s, conv_s, pool_s)
```

```python
import functools

import jax
import jax.numpy as jnp
from jax import lax
from jax.experimental import pallas as pl
from jax.experimental.pallas import tpu as pltpu

F32 = jnp.float32
BF16 = jnp.bfloat16

D_MODEL = 1024
DEPTH = 4
CHUNK = 64
HALF = CHUNK // 2
N_MEM = 256
HG_DK = 128
HG_WIDTH = 512
HG_HEADS = 4
CONV_WIDTH = 256
CONV_K = 31
CONV_HIST = CONV_K - 1
POOL_WIDTH = 256
POOL_HIST = 15
D_IN = 4 * HG_WIDTH + 2 * CONV_WIDTH + POOL_WIDTH
X_HEADS = 4
X_HDIM = 256
D_FF = 4096
EPS = 1e-6
F_FLOOR = 1e-30

Z_Q, Z_F, Z_V, Z_OG, Z_CA, Z_CG, Z_PU = 0, 512, 1024, 1536, 2048, 2304, 2560

SUBLANES = 8
CONV_PAD = 32
POOL_PAD = 24
CONV_SHIFT_ROWS = 8
PAST_LEN = 2048

SAFE_DECAY = 80.0

VMEM_LIMIT_V7X = 56 * 1024 * 1024

NT_DIMS = (((1,), (1,)), ((), ()))
TN_DIMS = (((0,), (0,)), ((), ()))


def _rms(x, g):
    ms = jnp.mean(x * x, axis=-1, keepdims=True)
    return x * lax.rsqrt(ms + EPS) * g


def _silu(x):
    return x * jax.nn.sigmoid(x)


def _dot(a, b):
    return jnp.dot(a, b, preferred_element_type=F32)


def _scores_factored(qh, kh, ah, row, col):
    lower = row < HALF
    amid = ah[HALF - 1:HALF, :]
    ref = jnp.where(lower, 0.0, amid)
    ql = qh * jnp.exp(ah - ref)
    kl = kh * jnp.exp(ref - ah)
    emid = jnp.exp(amid)
    k_first = jnp.where(lower, kl, 0.0).astype(BF16)
    k_second = jnp.where(lower, kl * emid, kl).astype(BF16)
    qlb = ql.astype(BF16)
    p0 = lax.dot_general(qlb[:HALF], k_first, NT_DIMS, preferred_element_type=F32)
    p1 = lax.dot_general(qlb[HALF:], k_second, NT_DIMS, preferred_element_type=F32)
    p = jnp.concatenate([p0, p1], axis=0)
    return jnp.where(row >= col, p, 0.0)


def _scores_exact(z_ref, a_ref, r0, hd, qh, ah, row):
    lane = lax.broadcasted_iota(jnp.int32, (1, HG_DK), 1)
    lo = hd * HG_DK

    def body(g, p):
        src = pl.ds(pl.multiple_of(r0 + g * SUBLANES, SUBLANES), SUBLANES)
        a_g = a_ref[src, lo:lo + HG_DK]
        k_g = z_ref[src, Z_F + lo:Z_F + lo + HG_DK]
        for i in range(SUBLANES):
            w = qh * k_g[i:i + 1, :] * jnp.exp(jnp.minimum(ah - a_g[i:i + 1, :], 0.0))
            p = jnp.where(lane == g * SUBLANES + i, jnp.sum(w, axis=-1, keepdims=True), p)
        return p

    p = lax.fori_loop(0, CHUNK // SUBLANES, body, jnp.zeros((CHUNK, HG_DK), F32))
    return jnp.where(row >= lane, p, 0.0)[:, :CHUNK]


def _mixer_kernel(li, pos0, tt,
                  x_ref, g_ref, win_ref, lbp_ref, hgg_ref, cw_ref, cb_ref, clg_ref, clb_ref,
                  pw_ref, ps_ref, wout_ref, hg0_ref, conv0_ref, pool0_ref,
                  y_ref, hg_out_ref, conv_out_ref, pool_out_ref,
                  z_ref, a_ref, mix_ref, dp_ref, st_ref, cfull_ref, c8_ref, pfull_ref, s2_ref, s4_ref, s8_ref):
    t = pl.program_id(1)
    nchunks = tt // CHUNK

    @pl.when(t == 0)
    def _():
        for hd in range(HG_HEADS):
            st_ref[hd] = hg0_ref[hd].T
        cfull_ref[pl.ds(CONV_PAD - CONV_HIST, CONV_HIST), :] = conv0_ref[...]
        pfull_ref[pl.ds(0, 2 * SUBLANES), :] = jnp.zeros((2 * SUBLANES, POOL_WIDTH), F32)
        s2_ref[pl.ds(0, SUBLANES), :] = jnp.zeros((SUBLANES, POOL_WIDTH), F32)
        s4_ref[pl.ds(0, SUBLANES), :] = jnp.zeros((SUBLANES, POOL_WIDTH), F32)
        pfull_ref[pl.ds(POOL_PAD - POOL_HIST, POOL_HIST), :] = pool0_ref[...]

    h = _rms(x_ref[...], g_ref[...]).astype(BF16)
    for lo in range(0, D_IN, 512):
        hi = min(lo + 512, D_IN)
        z_ref[:, lo:hi] = _dot(h, win_ref[:, lo:hi])

    lbp = lbp_ref[...]
    e = jnp.exp(lbp - jnp.max(lbp, axis=0, keepdims=True))
    sm = e / jnp.sum(e, axis=0, keepdims=True)
    lb = jnp.zeros((1, HG_WIDTH), F32)
    for j in range(1, li + 1):
        lb = lb + sm[j:j + 1, :]
    oml = 1.0 - lb

    lane = lax.broadcasted_iota(jnp.int32, (1, 128), 1)
    low_lanes = lane < 64

    cfull_ref[pl.ds(CONV_PAD, tt), :] = z_ref[:, Z_CA:Z_CG] * jax.nn.sigmoid(z_ref[:, Z_CG:Z_PU])
    for b in range(SUBLANES):
        n = tt + (CONV_K - 1 - b) // SUBLANES * SUBLANES
        c8_ref[b, pl.ds(CONV_SHIFT_ROWS, n), :] = cfull_ref[pl.ds(CONV_PAD - CONV_HIST + b, n), :]

    pfull_ref[pl.ds(POOL_PAD, tt), :] = z_ref[:, Z_PU:D_IN]
    n = tt + 2 * SUBLANES
    s2_ref[pl.ds(8, n), :] = pfull_ref[pl.ds(8, n), :] + pfull_ref[pl.ds(7, n), :]
    s4_ref[pl.ds(8, n), :] = s2_ref[pl.ds(8, n), :] + s2_ref[pl.ds(6, n), :]
    s8_ref[pl.ds(8, n), :] = s4_ref[pl.ds(8, n), 128:256] + s4_ref[pl.ds(4, n), 128:256]
    s16 = s8_ref[pl.ds(POOL_PAD, tt), :] + s8_ref[pl.ds(POOL_PAD - 8, tt), :]
    cur = pl.ds(POOL_PAD, tt)
    posf = (pos0 + 1 + t * tt + lax.broadcasted_iota(jnp.int32, (tt, 1), 0)).astype(F32)
    cnt_a = jnp.minimum(posf, jnp.where(low_lanes, 2.0, 4.0))
    cnt_b = jnp.minimum(posf, jnp.where(low_lanes, 8.0, 16.0))
    win_a = jnp.where(low_lanes, s2_ref[cur, 0:128], s4_ref[cur, 0:128])
    win_b = jnp.where(low_lanes, s8_ref[cur, :], s16)
    dp_ref[:, 0:128] = (win_a / cnt_a - pfull_ref[cur, 0:128]).astype(BF16)
    dp_ref[:, 128:256] = (win_b / cnt_b - pfull_ref[cur, 128:256]).astype(BF16)

    row = lax.broadcasted_iota(jnp.int32, (CHUNK, 1), 0)
    col = lax.broadcasted_iota(jnp.int32, (1, CHUNK), 1)
    tri = (row >= col).astype(BF16)

    def gate_body(c, dmax):
        r0 = pl.multiple_of(c * CHUNK, CHUNK)
        rows = pl.ds(r0, CHUNK)
        fl = z_ref[rows, Z_F:Z_V]
        f_gate = lb + oml * jax.nn.sigmoid(fl)
        log_f = jnp.log(jnp.maximum(f_gate, F_FLOOR))
        hi = log_f.astype(BF16)
        lo = (log_f - hi.astype(F32)).astype(BF16)
        cs = _dot(tri, jnp.concatenate([hi, lo], axis=1))
        a = cs[:, :HG_WIDTH] + cs[:, HG_WIDTH:]
        z_ref[rows, Z_F:Z_V] = oml * jax.nn.sigmoid(-fl)
        a_ref[rows, :] = a
        amid = a[HALF - 1:HALF, :]
        aend = a[CHUNK - 1:CHUNK, :]
        return jnp.maximum(dmax, jnp.maximum(-amid, amid - aend))

    dmax = lax.fori_loop(0, nchunks, gate_body, jnp.zeros((1, HG_WIDTH), F32))
    factorable = jnp.max(dmax) <= SAFE_DECAY

    hgg = hgg_ref[...]

    def chunk_body(factored, c, carry):
        r0 = pl.multiple_of(c * CHUNK, CHUNK)
        rows = pl.ds(r0, CHUNK)

        for hd in range(HG_HEADS):
            lo = hd * HG_DK
            qh = _silu(z_ref[rows, Z_Q + lo:Z_Q + lo + HG_DK])
            kh = z_ref[rows, Z_F + lo:Z_F + lo + HG_DK]
            vb = z_ref[rows, Z_V + lo:Z_V + lo + HG_DK].astype(BF16)
            ah = a_ref[rows, lo:lo + HG_DK]
            aend = ah[CHUNK - 1:CHUNK, :]
            if factored:
                p = _scores_factored(qh, kh, ah, row, col)
            else:
                p = _scores_exact(z_ref, a_ref, r0, hd, qh, ah, row)
            st = st_ref[hd]
            qa = (qh * jnp.exp(ah)).astype(BF16)
            o = (_dot(p.astype(BF16), vb)
                 + lax.dot_general(qa, st.astype(BF16), NT_DIMS, preferred_element_type=F32))
            ks = (kh * jnp.exp(aend - ah)).astype(BF16)
            st_ref[hd] = jnp.exp(aend) * st + lax.dot_general(vb, ks, TN_DIMS, preferred_element_type=F32)
            on = _rms(o, hgg[:, lo:lo + HG_DK])
            og = z_ref[rows, Z_OG + lo:Z_OG + lo + HG_DK]
            mix_ref[rows, lo:lo + HG_DK] = (on * _silu(og)).astype(BF16)

        dw = jnp.broadcast_to(cb_ref[...], (CHUNK, CONV_WIDTH))
        for j in range(CONV_K):
            a, b = divmod(j, SUBLANES)
            src = pl.multiple_of(r0 + (CONV_SHIFT_ROWS + SUBLANES * a), SUBLANES)
            dw = dw + c8_ref[b, pl.ds(src, CHUNK), :] * cw_ref[j:j + 1, :]
        mu = jnp.mean(dw, axis=-1, keepdims=True)
        xc = dw - mu
        var = jnp.mean(xc * xc, axis=-1, keepdims=True)
        yb = xc * lax.rsqrt(var + EPS) * clg_ref[...] + clb_ref[...]
        mix_ref[rows, HG_WIDTH:HG_WIDTH + CONV_WIDTH] = _silu(yb).astype(BF16)
        return carry

    def run(factored):
        lax.fori_loop(0, nchunks, functools.partial(chunk_body, factored), 0)

    lax.cond(factorable, lambda: run(True), lambda: run(False))

    c_out = _dot(dp_ref[...], pw_ref[...]) * ps_ref[...]
    mix_ref[:, HG_WIDTH + CONV_WIDTH:] = c_out.astype(BF16)
    y_ref[...] = x_ref[...] + _dot(mix_ref[...], wout_ref[...])

    @pl.when(t == pl.num_programs(1) - 1)
    def _():
        for hd in range(HG_HEADS):
            hg_out_ref[hd] = st_ref[hd].T
        conv_out_ref[...] = cfull_ref[pl.ds(tt + CONV_PAD - CONV_HIST, CONV_HIST), :]
        pool_out_ref[...] = pfull_ref[pl.ds(tt + POOL_PAD - POOL_HIST, POOL_HIST), :]

    cfull_ref[pl.ds(CONV_PAD - CONV_HIST, CONV_HIST), :] = cfull_ref[pl.ds(tt + CONV_PAD - CONV_HIST, CONV_HIST), :]
    pfull_ref[pl.ds(POOL_PAD - POOL_HIST, POOL_HIST), :] = pfull_ref[pl.ds(tt + POOL_PAD - POOL_HIST, POOL_HIST), :]


def _layer_spec(shape, li):
    zeros = (0,) * len(shape)
    return pl.BlockSpec((None,) + tuple(shape), lambda *_: (li,) + zeros)


def _mixer(li, pos0, x, hg0, conv0, pool0, p):
    b, t, _ = x.shape
    tt = min(t, 512)
    assert t % tt == 0 and tt % CHUNK == 0
    kern = functools.partial(_mixer_kernel, li, pos0, tt)
    per_stream = lambda shape: pl.BlockSpec((None,) + shape, lambda i, j: (i,) + (0,) * len(shape))
    return pl.pallas_call(
        kern,
        grid=(b, t // tt),
        in_specs=[
            pl.BlockSpec((None, tt, D_MODEL), lambda i, j: (i, j, 0)),
            _layer_spec((1, D_MODEL), li),
            _layer_spec((D_MODEL, D_IN), li),
            pl.BlockSpec((DEPTH, HG_WIDTH), lambda i, j: (0, 0)),
            _layer_spec((1, HG_WIDTH), li),
            _layer_spec((CONV_K, CONV_WIDTH), li),
            _layer_spec((1, CONV_WIDTH), li),
            _layer_spec((1, CONV_WIDTH), li),
            _layer_spec((1, CONV_WIDTH), li),
            _layer_spec((POOL_WIDTH, POOL_WIDTH), li),
            _layer_spec((1, POOL_WIDTH), li),
            _layer_spec((D_MODEL, D_MODEL), li),
            per_stream((HG_HEADS, HG_DK, HG_DK)),
            per_stream((CONV_HIST, CONV_WIDTH)),
            per_stream((POOL_HIST, POOL_WIDTH)),
        ],
        out_specs=[
            pl.BlockSpec((None, tt, D_MODEL), lambda i, j: (i, j, 0)),
            per_stream((HG_HEADS, HG_DK, HG_DK)),
            per_stream((CONV_HIST, CONV_WIDTH)),
            per_stream((POOL_HIST, POOL_WIDTH)),
        ],
        out_shape=[
            jax.ShapeDtypeStruct(x.shape, F32),
            jax.ShapeDtypeStruct((b, HG_HEADS, HG_DK, HG_DK), F32),
            jax.ShapeDtypeStruct((b, CONV_HIST, CONV_WIDTH), F32),
            jax.ShapeDtypeStruct((b, POOL_HIST, POOL_WIDTH), F32),
        ],
        scratch_shapes=[
            pltpu.VMEM((tt, D_IN), F32),
            pltpu.VMEM((tt, HG_WIDTH), F32),
            pltpu.VMEM((tt, D_MODEL), BF16),
            pltpu.VMEM((tt, POOL_WIDTH), BF16),
            pltpu.VMEM((HG_HEADS, HG_DK, HG_DK), F32),
            pltpu.VMEM((tt + CONV_PAD, CONV_WIDTH), F32),
            pltpu.VMEM((SUBLANES, tt + CONV_PAD, CONV_WIDTH), F32),
            pltpu.VMEM((tt + POOL_PAD, POOL_WIDTH), F32),
            pltpu.VMEM((tt + POOL_PAD, POOL_WIDTH), F32),
            pltpu.VMEM((tt + POOL_PAD, POOL_WIDTH), F32),
            pltpu.VMEM((tt + POOL_PAD, 128), F32),
        ],
        compiler_params=pltpu.CompilerParams(
            dimension_semantics=("arbitrary", "arbitrary"), vmem_limit_bytes=VMEM_LIMIT_V7X),
        name=f"mixer_l{li}",
    )(x, p["norm_mix_g"], p["w_in"], p["lb_param"], p["hg_norm_g"], p["conv_w"], p["conv_b"],
      p["conv_ln_g"], p["conv_ln_b"], p["pool_w_bd"], p["pool_scale"], p["w_out"], hg0, conv0, pool0)


def _xattn_kernel(x_ref, g_ref, wq_ref, wo_ref, k_ref, v_ref, y_ref):
    x = x_ref[...]
    h = _rms(x, g_ref[...]).astype(BF16)
    q = (_dot(h, wq_ref[...]) * (1.0 / 16.0)).astype(BF16)
    outs = []
    for hd in range(X_HEADS):
        lo = hd * X_HDIM
        s = lax.dot_general(q[:, lo:lo + X_HDIM], k_ref[:, lo:lo + X_HDIM], NT_DIMS,
                            preferred_element_type=F32)
        pexp = jnp.exp(s - jnp.max(s, axis=-1, keepdims=True))
        prob = pexp / jnp.sum(pexp, axis=-1, keepdims=True)
        outs.append(_dot(prob.astype(BF16), v_ref[:, lo:lo + X_HDIM]).astype(BF16))
    y_ref[...] = x + _dot(jnp.concatenate(outs, axis=1), wo_ref[...])


def _xattn(li, x, mem_k, mem_v, p):
    b, t, _ = x.shape
    tt = min(t, 512)
    kv_spec = pl.BlockSpec((None, None, N_MEM, D_MODEL), lambda i, j: (li, i, 0, 0))
    return pl.pallas_call(
        _xattn_kernel,
        grid=(b, t // tt),
        in_specs=[
            pl.BlockSpec((None, tt, D_MODEL), lambda i, j: (i, j, 0)),
            _layer_spec((1, D_MODEL), li),
            _layer_spec((D_MODEL, D_MODEL), li),
            _layer_spec((D_MODEL, D_MODEL), li),
            kv_spec, kv_spec,
        ],
        out_specs=pl.BlockSpec((None, tt, D_MODEL), lambda i, j: (i, j, 0)),
        out_shape=jax.ShapeDtypeStruct(x.shape, F32),
        compiler_params=pltpu.CompilerParams(
            dimension_semantics=("arbitrary", "arbitrary"), vmem_limit_bytes=VMEM_LIMIT_V7X),
        name=f"xattn_l{li}",
    )(x, p["norm_x_g"], p["xq_w"], p["xo_w"], mem_k, mem_v)


def _ffn_kernel(final, x_ref, g_ref, wup_ref, wdn_ref, fg_ref, y_ref):
    x = x_ref[...]
    h = _rms(x, g_ref[...]).astype(BF16)
    acc = x
    for lo in range(0, D_FF, 1024):
        u = jnp.maximum(_dot(h, wup_ref[:, lo:lo + 1024]), 0.0)
        acc = acc + _dot((u * u).astype(BF16), wdn_ref[lo:lo + 1024, :])
    y_ref[...] = _rms(acc, fg_ref[...]) if final else acc


def _ffn(li, x, p):
    b, t, _ = x.shape
    n = b * t
    tt = min(n, 512)
    resident = lambda shape: pl.BlockSpec((None,) + shape, lambda i: (li, 0, 0),
                                          pipeline_mode=pl.Buffered(1))
    y = pl.pallas_call(
        functools.partial(_ffn_kernel, li == DEPTH - 1),
        grid=(n // tt,),
        in_specs=[
            pl.BlockSpec((tt, D_MODEL), lambda i: (i, 0)),
            _layer_spec((1, D_MODEL), li),
            resident((D_MODEL, D_FF)),
            resident((D_FF, D_MODEL)),
            pl.BlockSpec((1, D_MODEL), lambda i: (0, 0)),
        ],
        out_specs=pl.BlockSpec((tt, D_MODEL), lambda i: (i, 0)),
        out_shape=jax.ShapeDtypeStruct((n, D_MODEL), F32),
        compiler_params=pltpu.CompilerParams(
            dimension_semantics=("arbitrary",), vmem_limit_bytes=VMEM_LIMIT_V7X),
        name=f"ffn_l{li}",
    )(x.reshape(n, D_MODEL), p["norm_ffn_g"], p["w_up"], p["w_down"], p["final_g"])
    return y.reshape(b, t, D_MODEL)


def _memkv_kernel(m_ref, g_ref, wk_ref, wv_ref, k_ref, v_ref, kb_ref, vb_ref):
    h = _rms(m_ref[...], g_ref[...]).astype(BF16)
    k = _dot(h, wk_ref[...])
    v = _dot(h, wv_ref[...])
    k_ref[...] = k
    v_ref[...] = v
    kb_ref[...] = k.astype(BF16)
    vb_ref[...] = v.astype(BF16)


def _memkv(mem, p):
    b = mem.shape[0]
    n = b * N_MEM
    rt = 1024
    out_spec = pl.BlockSpec((None, rt, D_MODEL), lambda l, r: (l, r, 0))
    w_spec = pl.BlockSpec((None, D_MODEL, D_MODEL), lambda l, r: (l, 0, 0))
    return pl.pallas_call(
        _memkv_kernel,
        grid=(DEPTH, n // rt),
        in_specs=[
            pl.BlockSpec((rt, D_MODEL), lambda l, r: (r, 0)),
            pl.BlockSpec((None, 1, D_MODEL), lambda l, r: (l, 0, 0)),
            w_spec, w_spec,
        ],
        out_specs=[out_spec] * 4,
        out_shape=[jax.ShapeDtypeStruct((DEPTH, n, D_MODEL), F32)] * 2
        + [jax.ShapeDtypeStruct((DEPTH, n, D_MODEL), BF16)] * 2,
        compiler_params=pltpu.CompilerParams(
            dimension_semantics=("arbitrary", "arbitrary"), vmem_limit_bytes=VMEM_LIMIT_V7X),
        name="memkv",
    )(mem.reshape(n, D_MODEL), p["norm_mem_g"], p["xk_w"], p["xv_w"])


def _trunk(x, mem_k, mem_v, hg_state, conv_buf, pool_buf, pos0, p):
    new_hg, new_conv, new_pool = [], [], []
    for li in range(DEPTH):
        x, s_hg, s_conv, s_pool = _mixer(li, pos0, x, hg_state[li], conv_buf[li], pool_buf[li], p)
        x = _xattn(li, x, mem_k, mem_v, p)
        x = _ffn(li, x, p)
        new_hg.append(s_hg)
        new_conv.append(s_conv)
        new_pool.append(s_pool)
    return x, jnp.stack(new_hg), jnp.stack(new_conv), jnp.stack(new_pool)


def kernel(x_prompt, x_sample, mem_prompt, state_hgrn, cache_conv, cache_pool, cache_mem_k, cache_mem_v,
           norm_mix_g, w_in, lb_param, hg_norm_g, conv_w, conv_b, conv_ln_g, conv_ln_b, pool_w, pool_scale,
           w_out, norm_x_g, norm_mem_g, xq_w, xk_w, xv_w, xo_w, norm_ffn_g, w_up, w_down, final_g):
    row = lambda a: a.reshape(a.shape[0], 1, a.shape[1])
    groups = pool_w.shape[1]
    pool_w_bd = (pool_w[:, :, :, None, :] * jnp.eye(groups, dtype=pool_w.dtype)[None, :, None, :, None]
                 ).reshape(DEPTH, POOL_WIDTH, POOL_WIDTH)
    p = dict(
        norm_mix_g=row(norm_mix_g), w_in=w_in.astype(BF16), lb_param=lb_param, hg_norm_g=row(hg_norm_g),
        conv_w=conv_w, conv_b=row(conv_b), conv_ln_g=row(conv_ln_g), conv_ln_b=row(conv_ln_b),
        pool_w_bd=pool_w_bd.astype(BF16), pool_scale=row(pool_scale), w_out=w_out.astype(BF16),
        norm_x_g=row(norm_x_g), norm_mem_g=row(norm_mem_g), xq_w=xq_w.astype(BF16), xk_w=xk_w.astype(BF16),
        xv_w=xv_w.astype(BF16), xo_w=xo_w.astype(BF16), norm_ffn_g=row(norm_ffn_g),
        w_up=w_up.astype(BF16), w_down=w_down.astype(BF16), final_g=final_g.reshape(1, D_MODEL))

    bp = x_prompt.shape[0]
    mem_k_p, mem_v_p, mem_kb, mem_vb = _memkv(mem_prompt, p)
    kv_shape = (DEPTH, bp, N_MEM, X_HEADS, X_HDIM)
    flat_kv = (DEPTH, bp, N_MEM, D_MODEL)

    dt = x_prompt.dtype
    hg0 = jnp.zeros((DEPTH, bp, HG_HEADS, HG_DK, HG_DK), dt)
    conv0 = jnp.zeros((DEPTH, bp, CONV_HIST, CONV_WIDTH), dt)
    pool0 = jnp.zeros((DEPTH, bp, POOL_HIST, POOL_WIDTH), dt)
    y_prompt, hg_p, conv_p, pool_p = _trunk(
        x_prompt, mem_kb.reshape(flat_kv), mem_vb.reshape(flat_kv), hg0, conv0, pool0, 0, p)

    bs = x_sample.shape[0]
    past_len = 2048
    y_sample, hg_s, conv_s, pool_s = _trunk(
        x_sample, cache_mem_k.reshape(DEPTH, bs, N_MEM, D_MODEL).astype(BF16),
        cache_mem_v.reshape(DEPTH, bs, N_MEM, D_MODEL).astype(BF16),
        state_hgrn, cache_conv, cache_pool, past_len, p)

    return (y_prompt, y_sample, hg_p, conv_p, pool_p, mem_k_p.reshape(kv_shape), mem_v_p.reshape(kv_shape),
            hg_s, conv_s, pool_s)
```

```python
import functools

import jax
import jax.numpy as jnp
from jax import lax
from jax.experimental import pallas as pl
from jax.experimental.pallas import tpu as pltpu

F32 = jnp.float32
BF16 = jnp.bfloat16

D_MODEL = 1024
DEPTH = 4
CHUNK = 64
HALF = CHUNK // 2
N_MEM = 256
HG_DK = 128
HG_WIDTH = 512
HG_HEADS = 4
CONV_WIDTH = 256
CONV_K = 31
CONV_HIST = CONV_K - 1
POOL_WIDTH = 256
POOL_HIST = 15
D_IN = 4 * HG_WIDTH + 2 * CONV_WIDTH + POOL_WIDTH
X_HEADS = 4
X_HDIM = 256
D_FF = 4096
EPS = 1e-6
F_FLOOR = 1e-30
PAST_LEN = 2048

Z_Q, Z_F, Z_V, Z_OG, Z_CA, Z_CG, Z_PU = 0, 512, 1024, 1536, 2048, 2304, 2560

SUBLANES = 8
CONV_PAD = 32
POOL_PAD = 24
CONV_SHIFT_ROWS = 8

SAFE_DECAY = 80.0

TOKEN_TILE = 256
FF_BLOCKS = 4
VMEM_LIMIT_V7X = 58 * 1024 * 1024

NT_DIMS = (((1,), (1,)), ((), ()))
TN_DIMS = (((0,), (0,)), ((), ()))


def _rms(x, g):
    ms = jnp.mean(x * x, axis=-1, keepdims=True)
    return x * lax.rsqrt(ms + EPS) * g


def _silu(x):
    return x * jax.nn.sigmoid(x)


def _dot(a, b):
    return jnp.dot(a, b, preferred_element_type=F32)


def _scores_factored(qh, kh, ah, row, col):
    lower = row < HALF
    amid = ah[HALF - 1:HALF, :]
    ref = jnp.where(lower, 0.0, amid)
    ql = qh * jnp.exp(ah - ref)
    kl = kh * jnp.exp(ref - ah)
    emid = jnp.exp(amid)
    k_first = jnp.where(lower, kl, 0.0).astype(BF16)
    k_second = jnp.where(lower, kl * emid, kl).astype(BF16)
    qlb = ql.astype(BF16)
    p0 = lax.dot_general(qlb[:HALF], k_first, NT_DIMS, preferred_element_type=F32)
    p1 = lax.dot_general(qlb[HALF:], k_second, NT_DIMS, preferred_element_type=F32)
    p = jnp.concatenate([p0, p1], axis=0)
    return jnp.where(row >= col, p, 0.0)


def _scores_exact(z_ref, a_ref, r0, hd, qh, ah, row):
    lane = lax.broadcasted_iota(jnp.int32, (1, HG_DK), 1)
    lo = hd * HG_DK

    def body(g, p):
        src = pl.ds(pl.multiple_of(r0 + g * SUBLANES, SUBLANES), SUBLANES)
        a_g = a_ref[src, lo:lo + HG_DK]
        k_g = z_ref[src, Z_F + lo:Z_F + lo + HG_DK]
        for i in range(SUBLANES):
            w = qh * k_g[i:i + 1, :] * jnp.exp(jnp.minimum(ah - a_g[i:i + 1, :], 0.0))
            p = jnp.where(lane == g * SUBLANES + i, jnp.sum(w, axis=-1, keepdims=True), p)
        return p

    p = lax.fori_loop(0, CHUNK // SUBLANES, body, jnp.zeros((CHUNK, HG_DK), F32))
    return jnp.where(row >= lane, p, 0.0)[:, :CHUNK]


def _layer_kernel(li, pos0, tt, nt, final,
                  x_ref, gmix_ref, win_ref, lbp_ref, hgg_ref, cw_ref, cb_ref, clg_ref, clb_ref,
                  pw_ref, ps_ref, wout_ref, hg0_ref, conv0_ref, pool0_ref,
                  gx_ref, wq_ref, wo_ref, k_ref, v_ref,
                  gffn_ref, wup_ref, wdn_ref, fg_ref,
                  y_ref, hg_out_ref, conv_out_ref, pool_out_ref,
                  z_ref, a_ref, mix_ref, dp_ref, st_ref, cfull_ref, c8_ref, pfull_ref, s2_ref, s4_ref, s8_ref):
    t = lax.rem(pl.program_id(0), nt)
    nchunks = tt // CHUNK

    @pl.when(t == 0)
    def _():
        for hd in range(HG_HEADS):
            st_ref[hd] = hg0_ref[hd].T
        cfull_ref[pl.ds(CONV_PAD - CONV_HIST, CONV_HIST), :] = conv0_ref[...]
        pfull_ref[pl.ds(0, 2 * SUBLANES), :] = jnp.zeros((2 * SUBLANES, POOL_WIDTH), F32)
        s2_ref[pl.ds(0, SUBLANES), :] = jnp.zeros((SUBLANES, POOL_WIDTH), F32)
        s4_ref[pl.ds(0, SUBLANES), :] = jnp.zeros((SUBLANES, POOL_WIDTH), F32)
        pfull_ref[pl.ds(POOL_PAD - POOL_HIST, POOL_HIST), :] = pool0_ref[...]

    h = _rms(x_ref[...], gmix_ref[...]).astype(BF16)
    for lo, hi in ((Z_F, Z_V), (Z_CA, Z_PU), (Z_PU, D_IN), (Z_Q, Z_F), (Z_V, Z_OG), (Z_OG, Z_CA)):
        z_ref[:, lo:hi] = _dot(h, win_ref[:, lo:hi])

    lbp = lbp_ref[...]
    e = jnp.exp(lbp - jnp.max(lbp, axis=0, keepdims=True))
    sm = e / jnp.sum(e, axis=0, keepdims=True)
    lb = jnp.zeros((1, HG_WIDTH), F32)
    for j in range(1, li + 1):
        lb = lb + sm[j:j + 1, :]
    oml = 1.0 - lb

    lane = lax.broadcasted_iota(jnp.int32, (1, 128), 1)
    low_lanes = lane < 64

    cfull_ref[pl.ds(CONV_PAD, tt), :] = z_ref[:, Z_CA:Z_CG] * jax.nn.sigmoid(z_ref[:, Z_CG:Z_PU])
    for b in range(SUBLANES):
        n = tt + (CONV_K - 1 - b) // SUBLANES * SUBLANES
        c8_ref[b, pl.ds(CONV_SHIFT_ROWS, n), :] = cfull_ref[pl.ds(CONV_PAD - CONV_HIST + b, n), :]

    for c in range(nchunks):
        r0 = c * CHUNK
        dw = jnp.broadcast_to(cb_ref[...], (CHUNK, CONV_WIDTH))
        for j in range(CONV_K):
            a, b = divmod(j, SUBLANES)
            dw = dw + c8_ref[b, pl.ds(r0 + CONV_SHIFT_ROWS + SUBLANES * a, CHUNK), :] * cw_ref[j:j + 1, :]
        mu = jnp.mean(dw, axis=-1, keepdims=True)
        xc = dw - mu
        var = jnp.mean(xc * xc, axis=-1, keepdims=True)
        yb = xc * lax.rsqrt(var + EPS) * clg_ref[...] + clb_ref[...]
        mix_ref[pl.ds(r0, CHUNK), HG_WIDTH:HG_WIDTH + CONV_WIDTH] = _silu(yb).astype(BF16)

    pfull_ref[pl.ds(POOL_PAD, tt), :] = z_ref[:, Z_PU:D_IN]
    n = tt + 2 * SUBLANES
    s2_ref[pl.ds(8, n), :] = pfull_ref[pl.ds(8, n), :] + pfull_ref[pl.ds(7, n), :]
    s4_ref[pl.ds(8, n), :] = s2_ref[pl.ds(8, n), :] + s2_ref[pl.ds(6, n), :]
    s8_ref[pl.ds(8, n), :] = s4_ref[pl.ds(8, n), 128:256] + s4_ref[pl.ds(4, n), 128:256]
    s16 = s8_ref[pl.ds(POOL_PAD, tt), :] + s8_ref[pl.ds(POOL_PAD - 8, tt), :]
    cur = pl.ds(POOL_PAD, tt)
    posf = (pos0 + 1 + t * tt + lax.broadcasted_iota(jnp.int32, (tt, 1), 0)).astype(F32)
    cnt_a = jnp.minimum(posf, jnp.where(low_lanes, 2.0, 4.0))
    cnt_b = jnp.minimum(posf, jnp.where(low_lanes, 8.0, 16.0))
    win_a = jnp.where(low_lanes, s2_ref[cur, 0:128], s4_ref[cur, 0:128])
    win_b = jnp.where(low_lanes, s8_ref[cur, :], s16)
    dp_ref[:, 0:128] = (win_a / cnt_a - pfull_ref[cur, 0:128]).astype(BF16)
    dp_ref[:, 128:256] = (win_b / cnt_b - pfull_ref[cur, 128:256]).astype(BF16)

    row = lax.broadcasted_iota(jnp.int32, (CHUNK, 1), 0)
    col = lax.broadcasted_iota(jnp.int32, (1, CHUNK), 1)
    tri = (row >= col).astype(BF16)

    dmax = jnp.zeros((1, HG_WIDTH), F32)
    for c in range(nchunks):
        rows = pl.ds(c * CHUNK, CHUNK)
        fl = z_ref[rows, Z_F:Z_V]
        f_gate = lb + oml * jax.nn.sigmoid(fl)
        log_f = jnp.log(jnp.maximum(f_gate, F_FLOOR))
        hi = log_f.astype(BF16)
        lo = (log_f - hi.astype(F32)).astype(BF16)
        cs = _dot(tri, jnp.concatenate([hi, lo], axis=1))
        a = cs[:, :HG_WIDTH] + cs[:, HG_WIDTH:]
        z_ref[rows, Z_F:Z_V] = oml * jax.nn.sigmoid(-fl)
        a_ref[rows, :] = a
        amid = a[HALF - 1:HALF, :]
        aend = a[CHUNK - 1:CHUNK, :]
        dmax = jnp.maximum(dmax, jnp.maximum(-amid, amid - aend))
    factorable = jnp.max(dmax) <= SAFE_DECAY

    hgg = hgg_ref[...]

    def chunk_body(factored, c, carry):
        r0 = pl.multiple_of(c * CHUNK, CHUNK)
        rows = pl.ds(r0, CHUNK)
        for hd in range(HG_HEADS):
            lo = hd * HG_DK
            qh = _silu(z_ref[rows, Z_Q + lo:Z_Q + lo + HG_DK])
            kh = z_ref[rows, Z_F + lo:Z_F + lo + HG_DK]
            vb = z_ref[rows, Z_V + lo:Z_V + lo + HG_DK].astype(BF16)
            ah = a_ref[rows, lo:lo + HG_DK]
            aend = ah[CHUNK - 1:CHUNK, :]
            if factored:
                p = _scores_factored(qh, kh, ah, row, col)
            else:
                p = _scores_exact(z_ref, a_ref, r0, hd, qh, ah, row)
            st = st_ref[hd]
            qa = (qh * jnp.exp(ah)).astype(BF16)
            o = (_dot(p.astype(BF16), vb)
                 + lax.dot_general(qa, st.astype(BF16), NT_DIMS, preferred_element_type=F32))
            ks = (kh * jnp.exp(aend - ah)).astype(BF16)
            st_ref[hd] = jnp.exp(aend) * st + lax.dot_general(vb, ks, TN_DIMS, preferred_element_type=F32)
            on = _rms(o, hgg[:, lo:lo + HG_DK])
            og = z_ref[rows, Z_OG + lo:Z_OG + lo + HG_DK]
            mix_ref[rows, lo:lo + HG_DK] = (on * _silu(og)).astype(BF16)
        return carry

    def run(factored):
        lax.fori_loop(0, nchunks, functools.partial(chunk_body, factored), 0)

    lax.cond(factorable, lambda: run(True), lambda: run(False))

    c_out = _dot(dp_ref[...], pw_ref[...]) * ps_ref[...]
    mix_ref[:, HG_WIDTH + CONV_WIDTH:] = c_out.astype(BF16)
    x1 = x_ref[...] + _dot(mix_ref[...], wout_ref[...])

    hx = _rms(x1, gx_ref[...]).astype(BF16)
    q = (_dot(hx, wq_ref[...]) * (1.0 / 16.0)).astype(BF16)
    outs = []
    for hd in range(X_HEADS):
        lo = hd * X_HDIM
        sc = lax.dot_general(q[:, lo:lo + X_HDIM], k_ref[:, lo:lo + X_HDIM], NT_DIMS,
                             preferred_element_type=F32)
        pexp = jnp.exp(sc - jnp.max(sc, axis=-1, keepdims=True))
        prob = pexp / jnp.sum(pexp, axis=-1, keepdims=True)
        outs.append(_dot(prob.astype(BF16), v_ref[:, lo:lo + X_HDIM]).astype(BF16))
    x2 = x1 + _dot(jnp.concatenate(outs, axis=1), wo_ref[...])

    hf = _rms(x2, gffn_ref[...]).astype(BF16)
    y = x2
    for c in range(wup_ref.shape[0]):
        u = jnp.maximum(_dot(hf, wup_ref[c]), 0.0)
        y = y + _dot((u * u).astype(BF16), wdn_ref[c])
    y_ref[...] = _rms(y, fg_ref[...]) if final else y

    @pl.when(t == nt - 1)
    def _():
        for hd in range(HG_HEADS):
            hg_out_ref[hd] = st_ref[hd].T
        conv_out_ref[...] = cfull_ref[pl.ds(tt + CONV_PAD - CONV_HIST, CONV_HIST), :]
        pool_out_ref[...] = pfull_ref[pl.ds(tt + POOL_PAD - POOL_HIST, POOL_HIST), :]

    cfull_ref[pl.ds(CONV_PAD - CONV_HIST, CONV_HIST), :] = cfull_ref[pl.ds(tt + CONV_PAD - CONV_HIST, CONV_HIST), :]
    pfull_ref[pl.ds(POOL_PAD - POOL_HIST, POOL_HIST), :] = pfull_ref[pl.ds(tt + POOL_PAD - POOL_HIST, POOL_HIST), :]


def _layer(li, pos0, x, mem_k, mem_v, hg0, conv0, pool0, p):
    b, t, _ = x.shape
    tt = min(t, TOKEN_TILE)
    assert t % tt == 0 and tt % CHUNK == 0
    nt = t // tt
    kern = functools.partial(_layer_kernel, li, pos0, tt, nt, li == DEPTH - 1)

    def resident(shape):
        zeros = (0,) * len(shape)
        return pl.BlockSpec((None,) + tuple(shape), lambda s: (li,) + zeros, pipeline_mode=pl.Buffered(1))

    def per_stream(shape):
        zeros = (0,) * len(shape)
        return pl.BlockSpec((None,) + tuple(shape), lambda s: (s // nt,) + zeros)

    kv_spec = pl.BlockSpec((None, None, N_MEM, D_MODEL), lambda s: (li, s // nt, 0, 0))
    return pl.pallas_call(
        kern,
        grid=(b * nt,),
        in_specs=[
            pl.BlockSpec((None, tt, D_MODEL), lambda s: (s // nt, s % nt, 0)),
            resident((1, D_MODEL)),
            resident((D_MODEL, D_IN)),
            pl.BlockSpec((DEPTH, HG_WIDTH), lambda s: (0, 0)),
            resident((1, HG_WIDTH)),
            resident((CONV_K, CONV_WIDTH)),
            resident((1, CONV_WIDTH)),
            resident((1, CONV_WIDTH)),
            resident((1, CONV_WIDTH)),
            resident((POOL_WIDTH, POOL_WIDTH)),
            resident((1, POOL_WIDTH)),
            resident((D_MODEL, D_MODEL)),
            per_stream((HG_HEADS, HG_DK, HG_DK)),
            per_stream((CONV_HIST, CONV_WIDTH)),
            per_stream((POOL_HIST, POOL_WIDTH)),
            resident((1, D_MODEL)),
            resident((D_MODEL, D_MODEL)),
            resident((D_MODEL, D_MODEL)),
            kv_spec, kv_spec,
            resident((1, D_MODEL)),
            resident((FF_BLOCKS, D_MODEL, D_FF // FF_BLOCKS)),
            resident((FF_BLOCKS, D_FF // FF_BLOCKS, D_MODEL)),
            pl.BlockSpec((1, D_MODEL), lambda s: (0, 0)),
        ],
        out_specs=[
            pl.BlockSpec((None, tt, D_MODEL), lambda s: (s // nt, s % nt, 0)),
            per_stream((HG_HEADS, HG_DK, HG_DK)),
            per_stream((CONV_HIST, CONV_WIDTH)),
            per_stream((POOL_HIST, POOL_WIDTH)),
        ],
        out_shape=[
            jax.ShapeDtypeStruct(x.shape, F32),
            jax.ShapeDtypeStruct((b, HG_HEADS, HG_DK, HG_DK), F32),
            jax.ShapeDtypeStruct((b, CONV_HIST, CONV_WIDTH), F32),
            jax.ShapeDtypeStruct((b, POOL_HIST, POOL_WIDTH), F32),
        ],
        scratch_shapes=[
            pltpu.VMEM((tt, D_IN), F32),
            pltpu.VMEM((tt, HG_WIDTH), F32),
            pltpu.VMEM((tt, D_MODEL), BF16),
            pltpu.VMEM((tt, POOL_WIDTH), BF16),
            pltpu.VMEM((HG_HEADS, HG_DK, HG_DK), F32),
            pltpu.VMEM((tt + CONV_PAD, CONV_WIDTH), F32),
            pltpu.VMEM((SUBLANES, tt + CONV_PAD, CONV_WIDTH), F32),
            pltpu.VMEM((tt + POOL_PAD, POOL_WIDTH), F32),
            pltpu.VMEM((tt + POOL_PAD, POOL_WIDTH), F32),
            pltpu.VMEM((tt + POOL_PAD, POOL_WIDTH), F32),
            pltpu.VMEM((tt + POOL_PAD, 128), F32),
        ],
        compiler_params=pltpu.CompilerParams(
            dimension_semantics=("arbitrary",), vmem_limit_bytes=VMEM_LIMIT_V7X),
        name=f"layer{li}",
    )(x, p["norm_mix_g"], p["w_in"], p["lb_param"], p["hg_norm_g"], p["conv_w"], p["conv_b"],
      p["conv_ln_g"], p["conv_ln_b"], p["pool_w_bd"], p["pool_scale"], p["w_out"], hg0, conv0, pool0,
      p["norm_x_g"], p["xq_w"], p["xo_w"], mem_k, mem_v,
      p["norm_ffn_g"], p["w_up"], p["w_down"], p["final_g"])


def _memkv_kernel(m_ref, g_ref, wk_ref, wv_ref, k_ref, v_ref, kb_ref, vb_ref):
    h = _rms(m_ref[...], g_ref[...]).astype(BF16)
    k = _dot(h, wk_ref[...])
    v = _dot(h, wv_ref[...])
    k_ref[...] = k
    v_ref[...] = v
    kb_ref[...] = k.astype(BF16)
    vb_ref[...] = v.astype(BF16)


def _memkv(mem, p):
    b = mem.shape[0]
    n = b * N_MEM
    rt = 1024
    out_spec = pl.BlockSpec((None, rt, D_MODEL), lambda l, r: (l, r, 0))
    w_spec = pl.BlockSpec((None, D_MODEL, D_MODEL), lambda l, r: (l, 0, 0))
    return pl.pallas_call(
        _memkv_kernel,
        grid=(DEPTH, n // rt),
        in_specs=[
            pl.BlockSpec((rt, D_MODEL), lambda l, r: (r, 0)),
            pl.BlockSpec((None, 1, D_MODEL), lambda l, r: (l, 0, 0)),
            w_spec, w_spec,
        ],
        out_specs=[out_spec] * 4,
        out_shape=[jax.ShapeDtypeStruct((DEPTH, n, D_MODEL), F32)] * 2
        + [jax.ShapeDtypeStruct((DEPTH, n, D_MODEL), BF16)] * 2,
        compiler_params=pltpu.CompilerParams(
            dimension_semantics=("arbitrary", "arbitrary"), vmem_limit_bytes=VMEM_LIMIT_V7X),
        name="memkv",
    )(mem.reshape(n, D_MODEL), p["norm_mem_g"], p["xk_w"], p["xv_w"])


def _trunk(x, mem_k, mem_v, hg_state, conv_buf, pool_buf, pos0, p):
    new_hg, new_conv, new_pool = [], [], []
    for li in range(DEPTH):
        x, s_hg, s_conv, s_pool = _layer(li, pos0, x, mem_k, mem_v, hg_state[li], conv_buf[li], pool_buf[li], p)
        new_hg.append(s_hg)
        new_conv.append(s_conv)
        new_pool.append(s_pool)
    return x, jnp.stack(new_hg), jnp.stack(new_conv), jnp.stack(new_pool)


def _ffn_blocks(w_up, w_down):
    blk = D_FF // FF_BLOCKS
    up = w_up.astype(BF16).reshape(DEPTH, D_MODEL, FF_BLOCKS, blk).transpose(0, 2, 1, 3)
    down = w_down.astype(BF16).reshape(DEPTH, FF_BLOCKS, blk, D_MODEL)
    return up, down


def kernel(x_prompt, x_sample, mem_prompt, state_hgrn, cache_conv, cache_pool, cache_mem_k, cache_mem_v,
           norm_mix_g, w_in, lb_param, hg_norm_g, conv_w, conv_b, conv_ln_g, conv_ln_b, pool_w, pool_scale,
           w_out, norm_x_g, norm_mem_g, xq_w, xk_w, xv_w, xo_w, norm_ffn_g, w_up, w_down, final_g):
    row = lambda a: a.reshape(a.shape[0], 1, a.shape[1])
    groups = pool_w.shape[1]
    pool_w_bd = (pool_w[:, :, :, None, :] * jnp.eye(groups, dtype=pool_w.dtype)[None, :, None, :, None]
                 ).reshape(DEPTH, POOL_WIDTH, POOL_WIDTH)
    w_up_blocks, w_down_blocks = _ffn_blocks(w_up, w_down)
    p = dict(
        norm_mix_g=row(norm_mix_g), w_in=w_in.astype(BF16), lb_param=lb_param, hg_norm_g=row(hg_norm_g),
        conv_w=conv_w, conv_b=row(conv_b), conv_ln_g=row(conv_ln_g), conv_ln_b=row(conv_ln_b),
        pool_w_bd=pool_w_bd.astype(BF16), pool_scale=row(pool_scale), w_out=w_out.astype(BF16),
        norm_x_g=row(norm_x_g), norm_mem_g=row(norm_mem_g), xq_w=xq_w.astype(BF16), xk_w=xk_w.astype(BF16),
        xv_w=xv_w.astype(BF16), xo_w=xo_w.astype(BF16), norm_ffn_g=row(norm_ffn_g),
        w_up=w_up_blocks, w_down=w_down_blocks,
        final_g=final_g.reshape(1, D_MODEL))

    bp = x_prompt.shape[0]
    mem_k_p, mem_v_p, mem_kb, mem_vb = _memkv(mem_prompt, p)
    kv_shape = (DEPTH, bp, N_MEM, X_HEADS, X_HDIM)
    flat_kv = (DEPTH, bp, N_MEM, D_MODEL)

    dt = x_prompt.dtype
    hg0 = jnp.zeros((DEPTH, bp, HG_HEADS, HG_DK, HG_DK), dt)
    conv0 = jnp.zeros((DEPTH, bp, CONV_HIST, CONV_WIDTH), dt)
    pool0 = jnp.zeros((DEPTH, bp, POOL_HIST, POOL_WIDTH), dt)
    y_prompt, hg_p, conv_p, pool_p = _trunk(
        x_prompt, mem_kb.reshape(flat_kv), mem_vb.reshape(flat_kv), hg0, conv0, pool0, 0, p)

    bs = x_sample.shape[0]
    y_sample, hg_s, conv_s, pool_s = _trunk(
        x_sample, cache_mem_k.reshape(DEPTH, bs, N_MEM, D_MODEL).astype(BF16),
        cache_mem_v.reshape(DEPTH, bs, N_MEM, D_MODEL).astype(BF16),
        state_hgrn, cache_conv, cache_pool, PAST_LEN, p)

    return (y_prompt, y_sample, hg_p, conv_p, pool_p, mem_k_p.reshape(kv_shape), mem_v_p.reshape(kv_shape),
            hg_s, conv_s, pool_s)
```

```python
import functools

import jax
import jax.numpy as jnp
from jax import lax
from jax.experimental import pallas as pl
from jax.experimental.pallas import tpu as pltpu

F32 = jnp.float32
BF16 = jnp.bfloat16

D_MODEL = 1024
DEPTH = 4
CHUNK = 64
HALF = CHUNK // 2
N_MEM = 256
HG_DK = 128
HG_WIDTH = 512
HG_HEADS = 4
CONV_WIDTH = 256
CONV_K = 31
CONV_HIST = CONV_K - 1
POOL_WIDTH = 256
POOL_HIST = 15
D_IN = 4 * HG_WIDTH + 2 * CONV_WIDTH + POOL_WIDTH
X_HEADS = 4
X_HDIM = 256
D_FF = 4096
EPS = 1e-6
F_FLOOR = 1e-30
PAST_LEN = 2048

Z_Q, Z_F, Z_V, Z_OG, Z_CA, Z_CG, Z_PU = 0, 512, 1024, 1536, 2048, 2304, 2560

SUBLANES = 8
CONV_PAD = 32
POOL_PAD = 24
CONV_SHIFT_ROWS = 8

SAFE_DECAY = 80.0

TOKEN_TILE = 256
FF_BLOCKS = 4
VMEM_LIMIT_V7X = 58 * 1024 * 1024

NT_DIMS = (((1,), (1,)), ((), ()))
TN_DIMS = (((0,), (0,)), ((), ()))


def _rms(x, g):
    ms = jnp.mean(x * x, axis=-1, keepdims=True)
    return x * lax.rsqrt(ms + EPS) * g


def _silu(x):
    return x * jax.nn.sigmoid(x)


def _dot(a, b):
    return jnp.dot(a, b, preferred_element_type=F32)


def _scores_factored(qh, kh, ah, row, col):
    lower = row < HALF
    amid = ah[HALF - 1:HALF, :]
    ref = jnp.where(lower, 0.0, amid)
    ql = qh * jnp.exp(ah - ref)
    kl = kh * jnp.exp(ref - ah)
    emid = jnp.exp(amid)
    k_first = jnp.where(lower, kl, 0.0).astype(BF16)
    k_second = jnp.where(lower, kl * emid, kl).astype(BF16)
    qlb = ql.astype(BF16)
    p0 = lax.dot_general(qlb[:HALF], k_first, NT_DIMS, preferred_element_type=F32)
    p1 = lax.dot_general(qlb[HALF:], k_second, NT_DIMS, preferred_element_type=F32)
    p = jnp.concatenate([p0, p1], axis=0)
    return jnp.where(row >= col, p, 0.0)


def _scores_exact(z_ref, a_ref, r0, hd, qh, ah, row):
    lane = lax.broadcasted_iota(jnp.int32, (1, HG_DK), 1)
    lo = hd * HG_DK

    def body(g, p):
        src = pl.ds(pl.multiple_of(r0 + g * SUBLANES, SUBLANES), SUBLANES)
        a_g = a_ref[src, lo:lo + HG_DK]
        k_g = z_ref[src, Z_F + lo:Z_F + lo + HG_DK]
        for i in range(SUBLANES):
            w = qh * k_g[i:i + 1, :] * jnp.exp(jnp.minimum(ah - a_g[i:i + 1, :], 0.0))
            p = jnp.where(lane == g * SUBLANES + i, jnp.sum(w, axis=-1, keepdims=True), p)
        return p

    p = lax.fori_loop(0, CHUNK // SUBLANES, body, jnp.zeros((CHUNK, HG_DK), F32))
    return jnp.where(row >= lane, p, 0.0)[:, :CHUNK]


def _layer_kernel(li, pos0, tt, nt, ns, final,
                  xn_ref, xc_ref, gmix_ref, win_ref, lbp_ref, hgg_ref, cw_ref, cb_ref, clg_ref, clb_ref,
                  pw_ref, ps_ref, wout_ref, hg0_ref, conv0_ref, pool0_ref,
                  gx_ref, wq_ref, wo_ref, k_ref, v_ref,
                  gffn_ref, wup_ref, wdn_ref, fg_ref,
                  y_ref, hg_out_ref, conv_out_ref, pool_out_ref,
                  z_ref, a_ref, mix_ref, dp_ref, st_ref, cfull_ref, c8_ref, pfull_ref, s2_ref, s4_ref, s8_ref,
                  p_ref, qa_ref, ks_ref, vb_ref, sog_ref, dec_ref):
    j = pl.program_id(0)
    tn = lax.rem(jnp.minimum(j, ns - 1), nt)
    tp = lax.rem(jnp.maximum(j - 1, 0), nt)
    nchunks = tt // CHUNK

    @pl.when(j == 0)
    def _():
        for ref in (p_ref, qa_ref, ks_ref, vb_ref, sog_ref, dec_ref, mix_ref, dp_ref):
            ref[...] = jnp.zeros(ref.shape, ref.dtype)

    @pl.when(tp == 0)
    def _():
        for hd in range(HG_HEADS):
            st_ref[hd] = hg0_ref[hd].T

    @pl.when(tn == 0)
    def _():
        cfull_ref[pl.ds(CONV_PAD - CONV_HIST, CONV_HIST), :] = conv0_ref[...]
        pfull_ref[pl.ds(0, 2 * SUBLANES), :] = jnp.zeros((2 * SUBLANES, POOL_WIDTH), F32)
        s2_ref[pl.ds(0, SUBLANES), :] = jnp.zeros((SUBLANES, POOL_WIDTH), F32)
        s4_ref[pl.ds(0, SUBLANES), :] = jnp.zeros((SUBLANES, POOL_WIDTH), F32)
        pfull_ref[pl.ds(POOL_PAD - POOL_HIST, POOL_HIST), :] = pool0_ref[...]

    hgg = hgg_ref[...]
    for c in range(nchunks):
        rows = pl.ds(c * CHUNK, CHUNK)
        for hd in range(HG_HEADS):
            cols = slice(hd * HG_DK, (hd + 1) * HG_DK)
            st = st_ref[hd]
            vb = vb_ref[rows, cols]
            o = (_dot(p_ref[hd, rows, :], vb)
                 + lax.dot_general(qa_ref[rows, cols], st.astype(BF16), NT_DIMS, preferred_element_type=F32))
            st_ref[hd] = dec_ref[c, :, cols] * st + lax.dot_general(vb, ks_ref[rows, cols], TN_DIMS,
                                                                     preferred_element_type=F32)
            mix_ref[rows, cols] = (_rms(o, hgg[:, cols]) * sog_ref[rows, cols]).astype(BF16)

    c_out = _dot(dp_ref[...], pw_ref[...]) * ps_ref[...]
    mix_ref[:, HG_WIDTH + CONV_WIDTH:] = c_out.astype(BF16)
    x1 = xc_ref[...] + _dot(mix_ref[...], wout_ref[...])

    h = _rms(xn_ref[...], gmix_ref[...]).astype(BF16)
    for lo, hi in ((Z_F, Z_V), (Z_CA, Z_PU), (Z_PU, D_IN), (Z_Q, Z_F), (Z_V, Z_OG), (Z_OG, Z_CA)):
        z_ref[:, lo:hi] = _dot(h, win_ref[:, lo:hi])

    hx = _rms(x1, gx_ref[...]).astype(BF16)
    q = (_dot(hx, wq_ref[...]) * (1.0 / 16.0)).astype(BF16)
    outs = []
    for hd in range(X_HEADS):
        lo = hd * X_HDIM
        sc = lax.dot_general(q[:, lo:lo + X_HDIM], k_ref[:, lo:lo + X_HDIM], NT_DIMS,
                             preferred_element_type=F32)
        pexp = jnp.exp(sc - jnp.max(sc, axis=-1, keepdims=True))
        prob = pexp / jnp.sum(pexp, axis=-1, keepdims=True)
        outs.append(_dot(prob.astype(BF16), v_ref[:, lo:lo + X_HDIM]).astype(BF16))
    x2 = x1 + _dot(jnp.concatenate(outs, axis=1), wo_ref[...])

    hf = _rms(x2, gffn_ref[...]).astype(BF16)

    def ffn_block(y, c):
        u = jnp.maximum(_dot(hf, wup_ref[c]), 0.0)
        return y + _dot((u * u).astype(BF16), wdn_ref[c])

    y = x2
    for c in range(FF_BLOCKS - 2):
        y = ffn_block(y, c)

    lbp = lbp_ref[...]
    e = jnp.exp(lbp - jnp.max(lbp, axis=0, keepdims=True))
    sm = e / jnp.sum(e, axis=0, keepdims=True)
    lb = jnp.zeros((1, HG_WIDTH), F32)
    for i in range(1, li + 1):
        lb = lb + sm[i:i + 1, :]
    oml = 1.0 - lb

    lane = lax.broadcasted_iota(jnp.int32, (1, 128), 1)
    low_lanes = lane < 64

    cfull_ref[pl.ds(CONV_PAD, tt), :] = z_ref[:, Z_CA:Z_CG] * jax.nn.sigmoid(z_ref[:, Z_CG:Z_PU])
    for b in range(SUBLANES):
        n = tt + (CONV_K - 1 - b) // SUBLANES * SUBLANES
        c8_ref[b, pl.ds(CONV_SHIFT_ROWS, n), :] = cfull_ref[pl.ds(CONV_PAD - CONV_HIST + b, n), :]

    for c in range(nchunks):
        r0 = c * CHUNK
        dw = jnp.broadcast_to(cb_ref[...], (CHUNK, CONV_WIDTH))
        for tap in range(CONV_K):
            a, b = divmod(tap, SUBLANES)
            dw = dw + c8_ref[b, pl.ds(r0 + CONV_SHIFT_ROWS + SUBLANES * a, CHUNK), :] * cw_ref[tap:tap + 1, :]
        mu = jnp.mean(dw, axis=-1, keepdims=True)
        xc = dw - mu
        var = jnp.mean(xc * xc, axis=-1, keepdims=True)
        yb = xc * lax.rsqrt(var + EPS) * clg_ref[...] + clb_ref[...]
        mix_ref[pl.ds(r0, CHUNK), HG_WIDTH:HG_WIDTH + CONV_WIDTH] = _silu(yb).astype(BF16)

    pfull_ref[pl.ds(POOL_PAD, tt), :] = z_ref[:, Z_PU:D_IN]
    n = tt + 2 * SUBLANES
    s2_ref[pl.ds(8, n), :] = pfull_ref[pl.ds(8, n), :] + pfull_ref[pl.ds(7, n), :]
    s4_ref[pl.ds(8, n), :] = s2_ref[pl.ds(8, n), :] + s2_ref[pl.ds(6, n), :]
    s8_ref[pl.ds(8, n), :] = s4_ref[pl.ds(8, n), 128:256] + s4_ref[pl.ds(4, n), 128:256]
    s16 = s8_ref[pl.ds(POOL_PAD, tt), :] + s8_ref[pl.ds(POOL_PAD - 8, tt), :]
    cur = pl.ds(POOL_PAD, tt)
    posf = (pos0 + 1 + tn * tt + lax.broadcasted_iota(jnp.int32, (tt, 1), 0)).astype(F32)
    cnt_a = jnp.minimum(posf, jnp.where(low_lanes, 2.0, 4.0))
    cnt_b = jnp.minimum(posf, jnp.where(low_lanes, 8.0, 16.0))
    win_a = jnp.where(low_lanes, s2_ref[cur, 0:128], s4_ref[cur, 0:128])
    win_b = jnp.where(low_lanes, s8_ref[cur, :], s16)
    dp_ref[:, 0:128] = (win_a / cnt_a - pfull_ref[cur, 0:128]).astype(BF16)
    dp_ref[:, 128:256] = (win_b / cnt_b - pfull_ref[cur, 128:256]).astype(BF16)

    row = lax.broadcasted_iota(jnp.int32, (CHUNK, 1), 0)
    col = lax.broadcasted_iota(jnp.int32, (1, CHUNK), 1)
    tri = (row >= col).astype(BF16)
    dmax = jnp.zeros((1, HG_WIDTH), F32)
    for c in range(nchunks):
        rows = pl.ds(c * CHUNK, CHUNK)
        fl = z_ref[rows, Z_F:Z_V]
        f_gate = lb + oml * jax.nn.sigmoid(fl)
        log_f = jnp.log(jnp.maximum(f_gate, F_FLOOR))
        hi = log_f.astype(BF16)
        lo = (log_f - hi.astype(F32)).astype(BF16)
        cs = _dot(tri, jnp.concatenate([hi, lo], axis=1))
        a = cs[:, :HG_WIDTH] + cs[:, HG_WIDTH:]
        k = oml * jax.nn.sigmoid(-fl)
        z_ref[rows, Z_F:Z_V] = k
        a_ref[rows, :] = a
        amid = a[HALF - 1:HALF, :]
        aend = a[CHUNK - 1:CHUNK, :]
        dmax = jnp.maximum(dmax, jnp.maximum(-amid, amid - aend))
    unsafe = jnp.max(dmax) > SAFE_DECAY

    y = ffn_block(y, FF_BLOCKS - 2)

    for c in range(nchunks):
        rows = pl.ds(c * CHUNK, CHUNK)
        a = a_ref[rows, :]
        k = z_ref[rows, Z_F:Z_V]
        aend = a[CHUNK - 1:CHUNK, :]
        qs = _silu(z_ref[rows, Z_Q:Z_F])
        qa_ref[rows, :] = (qs * jnp.exp(a)).astype(BF16)
        ks_ref[rows, :] = (k * jnp.exp(aend - a)).astype(BF16)
        dec_ref[c] = jnp.exp(aend)
        vb_ref[rows, :] = z_ref[rows, Z_V:Z_OG].astype(BF16)
        sog_ref[rows, :] = _silu(z_ref[rows, Z_OG:Z_CA])
        for hd in range(HG_HEADS):
            cols = slice(hd * HG_DK, (hd + 1) * HG_DK)
            p_ref[hd, rows, :] = _scores_factored(qs[:, cols], k[:, cols], a[:, cols], row, col).astype(BF16)

    y = ffn_block(y, FF_BLOCKS - 1)
    y_ref[...] = _rms(y, fg_ref[...]) if final else y

    @pl.when(unsafe)
    def _():
        for c in range(nchunks):
            r0 = c * CHUNK
            rows = pl.ds(r0, CHUNK)
            for hd in range(HG_HEADS):
                lo = hd * HG_DK
                qh = _silu(z_ref[rows, Z_Q + lo:Z_Q + lo + HG_DK])
                ah = a_ref[rows, lo:lo + HG_DK]
                p_ref[hd, rows, :] = _scores_exact(z_ref, a_ref, r0, hd, qh, ah, row).astype(BF16)

    @pl.when((tp == nt - 1) & (j >= 1))
    def _():
        for hd in range(HG_HEADS):
            hg_out_ref[hd] = st_ref[hd].T

    @pl.when((tn == nt - 1) & (j < ns))
    def _():
        conv_out_ref[...] = cfull_ref[pl.ds(tt + CONV_PAD - CONV_HIST, CONV_HIST), :]
        pool_out_ref[...] = pfull_ref[pl.ds(tt + POOL_PAD - POOL_HIST, POOL_HIST), :]

    cfull_ref[pl.ds(CONV_PAD - CONV_HIST, CONV_HIST), :] = cfull_ref[pl.ds(tt + CONV_PAD - CONV_HIST, CONV_HIST), :]
    pfull_ref[pl.ds(POOL_PAD - POOL_HIST, POOL_HIST), :] = pfull_ref[pl.ds(tt + POOL_PAD - POOL_HIST, POOL_HIST), :]


def _layer(li, pos0, x, mem_k, mem_v, hg0, conv0, pool0, p):
    b, t, _ = x.shape
    tt = min(t, TOKEN_TILE)
    assert t % tt == 0 and tt % CHUNK == 0
    nt = t // tt
    ns = b * nt
    nchunks = tt // CHUNK
    kern = functools.partial(_layer_kernel, li, pos0, tt, nt, ns, li == DEPTH - 1)

    front = lambda j: jnp.minimum(j, ns - 1)
    back = lambda j: jnp.maximum(j - 1, 0)

    def resident(shape):
        zeros = (0,) * len(shape)
        return pl.BlockSpec((None,) + tuple(shape), lambda j: (li,) + zeros, pipeline_mode=pl.Buffered(1))

    def per_stream(shape, tile):
        zeros = (0,) * len(shape)
        return pl.BlockSpec((None,) + tuple(shape), lambda j: (tile(j) // nt,) + zeros)

    def tile_spec(tile):
        return pl.BlockSpec((None, tt, D_MODEL), lambda j: (tile(j) // nt, tile(j) % nt, 0))

    kv_spec = pl.BlockSpec((None, None, N_MEM, D_MODEL), lambda j: (li, back(j) // nt, 0, 0))
    return pl.pallas_call(
        kern,
        grid=(ns + 1,),
        in_specs=[
            tile_spec(front), tile_spec(back),
            resident((1, D_MODEL)),
            resident((D_MODEL, D_IN)),
            pl.BlockSpec((DEPTH, HG_WIDTH), lambda j: (0, 0)),
            resident((1, HG_WIDTH)),
            resident((CONV_K, CONV_WIDTH)),
            resident((1, CONV_WIDTH)),
            resident((1, CONV_WIDTH)),
            resident((1, CONV_WIDTH)),
            resident((POOL_WIDTH, POOL_WIDTH)),
            resident((1, POOL_WIDTH)),
            resident((D_MODEL, D_MODEL)),
            per_stream((HG_HEADS, HG_DK, HG_DK), back),
            per_stream((CONV_HIST, CONV_WIDTH), front),
            per_stream((POOL_HIST, POOL_WIDTH), front),
            resident((1, D_MODEL)),
            resident((D_MODEL, D_MODEL)),
            resident((D_MODEL, D_MODEL)),
            kv_spec, kv_spec,
            resident((1, D_MODEL)),
            resident((FF_BLOCKS, D_MODEL, D_FF // FF_BLOCKS)),
            resident((FF_BLOCKS, D_FF // FF_BLOCKS, D_MODEL)),
            pl.BlockSpec((1, D_MODEL), lambda j: (0, 0)),
        ],
        out_specs=[
            tile_spec(back),
            per_stream((HG_HEADS, HG_DK, HG_DK), back),
            per_stream((CONV_HIST, CONV_WIDTH), front),
            per_stream((POOL_HIST, POOL_WIDTH), front),
        ],
        out_shape=[
            jax.ShapeDtypeStruct(x.shape, F32),
            jax.ShapeDtypeStruct((b, HG_HEADS, HG_DK, HG_DK), F32),
            jax.ShapeDtypeStruct((b, CONV_HIST, CONV_WIDTH), F32),
            jax.ShapeDtypeStruct((b, POOL_HIST, POOL_WIDTH), F32),
        ],
        scratch_shapes=[
            pltpu.VMEM((tt, D_IN), F32),
            pltpu.VMEM((tt, HG_WIDTH), F32),
            pltpu.VMEM((tt, D_MODEL), BF16),
            pltpu.VMEM((tt, POOL_WIDTH), BF16),
            pltpu.VMEM((HG_HEADS, HG_DK, HG_DK), F32),
            pltpu.VMEM((tt + CONV_PAD, CONV_WIDTH), F32),
            pltpu.VMEM((SUBLANES, tt + CONV_PAD, CONV_WIDTH), F32),
            pltpu.VMEM((tt + POOL_PAD, POOL_WIDTH), F32),
            pltpu.VMEM((tt + POOL_PAD, POOL_WIDTH), F32),
            pltpu.VMEM((tt + POOL_PAD, POOL_WIDTH), F32),
            pltpu.VMEM((tt + POOL_PAD, 128), F32),
            pltpu.VMEM((HG_HEADS, tt, CHUNK), BF16),
            pltpu.VMEM((tt, HG_WIDTH), BF16),
            pltpu.VMEM((tt, HG_WIDTH), BF16),
            pltpu.VMEM((tt, HG_WIDTH), BF16),
            pltpu.VMEM((tt, HG_WIDTH), F32),
            pltpu.VMEM((nchunks, 1, HG_WIDTH), F32),
        ],
        compiler_params=pltpu.CompilerParams(
            dimension_semantics=("arbitrary",), vmem_limit_bytes=VMEM_LIMIT_V7X),
        name=f"layer{li}",
    )(x, x, p["norm_mix_g"], p["w_in"], p["lb_param"], p["hg_norm_g"], p["conv_w"], p["conv_b"],
      p["conv_ln_g"], p["conv_ln_b"], p["pool_w_bd"], p["pool_scale"], p["w_out"], hg0, conv0, pool0,
      p["norm_x_g"], p["xq_w"], p["xo_w"], mem_k, mem_v,
      p["norm_ffn_g"], p["w_up"], p["w_down"], p["final_g"])


def _memkv_kernel(m_ref, g_ref, wk_ref, wv_ref, k_ref, v_ref, kb_ref, vb_ref):
    h = _rms(m_ref[...], g_ref[...]).astype(BF16)
    k = _dot(h, wk_ref[...])
    v = _dot(h, wv_ref[...])
    k_ref[...] = k
    v_ref[...] = v
    kb_ref[...] = k.astype(BF16)
    vb_ref[...] = v.astype(BF16)


def _memkv(mem, p):
    b = mem.shape[0]
    n = b * N_MEM
    rt = 1024
    out_spec = pl.BlockSpec((None, rt, D_MODEL), lambda l, r: (l, r, 0))
    w_spec = pl.BlockSpec((None, D_MODEL, D_MODEL), lambda l, r: (l, 0, 0))
    return pl.pallas_call(
        _memkv_kernel,
        grid=(DEPTH, n // rt),
        in_specs=[
            pl.BlockSpec((rt, D_MODEL), lambda l, r: (r, 0)),
            pl.BlockSpec((None, 1, D_MODEL), lambda l, r: (l, 0, 0)),
            w_spec, w_spec,
        ],
        out_specs=[out_spec] * 4,
        out_shape=[jax.ShapeDtypeStruct((DEPTH, n, D_MODEL), F32)] * 2
        + [jax.ShapeDtypeStruct((DEPTH, n, D_MODEL), BF16)] * 2,
        compiler_params=pltpu.CompilerParams(
            dimension_semantics=("arbitrary", "arbitrary"), vmem_limit_bytes=VMEM_LIMIT_V7X),
        name="memkv",
    )(mem.reshape(n, D_MODEL), p["norm_mem_g"], p["xk_w"], p["xv_w"])


def _trunk(x, mem_k, mem_v, hg_state, conv_buf, pool_buf, pos0, p):
    new_hg, new_conv, new_pool = [], [], []
    for li in range(DEPTH):
        x, s_hg, s_conv, s_pool = _layer(li, pos0, x, mem_k, mem_v, hg_state[li], conv_buf[li], pool_buf[li], p)
        new_hg.append(s_hg)
        new_conv.append(s_conv)
        new_pool.append(s_pool)
    return x, jnp.stack(new_hg), jnp.stack(new_conv), jnp.stack(new_pool)


def _ffn_blocks(w_up, w_down):
    blk = D_FF // FF_BLOCKS
    up = w_up.astype(BF16).reshape(DEPTH, D_MODEL, FF_BLOCKS, blk).transpose(0, 2, 1, 3)
    down = w_down.astype(BF16).reshape(DEPTH, FF_BLOCKS, blk, D_MODEL)
    return up, down


def kernel(x_prompt, x_sample, mem_prompt, state_hgrn, cache_conv, cache_pool, cache_mem_k, cache_mem_v,
           norm_mix_g, w_in, lb_param, hg_norm_g, conv_w, conv_b, conv_ln_g, conv_ln_b, pool_w, pool_scale,
           w_out, norm_x_g, norm_mem_g, xq_w, xk_w, xv_w, xo_w, norm_ffn_g, w_up, w_down, final_g):
    row = lambda a: a.reshape(a.shape[0], 1, a.shape[1])
    groups = pool_w.shape[1]
    pool_w_bd = (pool_w[:, :, :, None, :] * jnp.eye(groups, dtype=pool_w.dtype)[None, :, None, :, None]
                 ).reshape(DEPTH, POOL_WIDTH, POOL_WIDTH)
    w_up_blocks, w_down_blocks = _ffn_blocks(w_up, w_down)
    p = dict(
        norm_mix_g=row(norm_mix_g), w_in=w_in.astype(BF16), lb_param=lb_param, hg_norm_g=row(hg_norm_g),
        conv_w=conv_w, conv_b=row(conv_b), conv_ln_g=row(conv_ln_g), conv_ln_b=row(conv_ln_b),
        pool_w_bd=pool_w_bd.astype(BF16), pool_scale=row(pool_scale), w_out=w_out.astype(BF16),
        norm_x_g=row(norm_x_g), norm_mem_g=row(norm_mem_g), xq_w=xq_w.astype(BF16), xk_w=xk_w.astype(BF16),
        xv_w=xv_w.astype(BF16), xo_w=xo_w.astype(BF16), norm_ffn_g=row(norm_ffn_g),
        w_up=w_up_blocks, w_down=w_down_blocks,
        final_g=final_g.reshape(1, D_MODEL))

    bp = x_prompt.shape[0]
    mem_k_p, mem_v_p, mem_kb, mem_vb = _memkv(mem_prompt, p)
    kv_shape = (DEPTH, bp, N_MEM, X_HEADS, X_HDIM)
    flat_kv = (DEPTH, bp, N_MEM, D_MODEL)

    dt = x_prompt.dtype
    hg0 = jnp.zeros((DEPTH, bp, HG_HEADS, HG_DK, HG_DK), dt)
    conv0 = jnp.zeros((DEPTH, bp, CONV_HIST, CONV_WIDTH), dt)
    pool0 = jnp.zeros((DEPTH, bp, POOL_HIST, POOL_WIDTH), dt)
    y_prompt, hg_p, conv_p, pool_p = _trunk(
        x_prompt, mem_kb.reshape(flat_kv), mem_vb.reshape(flat_kv), hg0, conv0, pool0, 0, p)

    bs = x_sample.shape[0]
    y_sample, hg_s, conv_s, pool_s = _trunk(
        x_sample, cache_mem_k.reshape(DEPTH, bs, N_MEM, D_MODEL).astype(BF16),
        cache_mem_v.reshape(DEPTH, bs, N_MEM, D_MODEL).astype(BF16),
        state_hgrn, cache_conv, cache_pool, PAST_LEN, p)

    return (y_prompt, y_sample, hg_p, conv_p, pool_p, mem_k_p.reshape(kv_shape), mem_v_p.reshape(kv_shape),
            hg_s, conv_s, pool_s)
```

```python
import functools

import jax
import jax.numpy as jnp
from jax import lax
from jax.experimental import pallas as pl
from jax.experimental.pallas import tpu as pltpu

F32 = jnp.float32
BF16 = jnp.bfloat16

D_MODEL = 1024
DEPTH = 4
CHUNK = 64
HALF = CHUNK // 2
N_MEM = 256
HG_DK = 128
HG_WIDTH = 512
HG_HEADS = 4
CONV_WIDTH = 256
CONV_K = 31
CONV_HIST = CONV_K - 1
POOL_WIDTH = 256
POOL_HIST = 15
D_IN = 4 * HG_WIDTH + 2 * CONV_WIDTH + POOL_WIDTH
X_HEADS = 4
X_HDIM = 256
D_FF = 4096
EPS = 1e-6
F_FLOOR = 1e-30
PAST_LEN = 2048

Z_Q, Z_F, Z_V, Z_OG, Z_CA, Z_CG, Z_PU = 0, 512, 1024, 1536, 2048, 2304, 2560

SUBLANES = 8
CONV_PAD = 32
POOL_PAD = 24
CONV_SHIFT_ROWS = 8

SAFE_DECAY = 80.0

TOKEN_TILE = 256
FF_BLOCKS = 4
VMEM_LIMIT_V7X = 58 * 1024 * 1024

NT_DIMS = (((1,), (1,)), ((), ()))
TN_DIMS = (((0,), (0,)), ((), ()))


def _rms(x, g):
    ms = jnp.mean(x * x, axis=-1, keepdims=True)
    return x * lax.rsqrt(ms + EPS) * g


def _silu(x):
    return x * jax.nn.sigmoid(x)


def _dot(a, b):
    return jnp.dot(a, b, preferred_element_type=F32)


def _scores_factored(qh, kh, ah, row, col):
    lower = row < HALF
    amid = ah[HALF - 1:HALF, :]
    ref = jnp.where(lower, 0.0, amid)
    ql = qh * jnp.exp(ah - ref)
    kl = kh * jnp.exp(ref - ah)
    emid = jnp.exp(amid)
    k_first = jnp.where(lower, kl, 0.0).astype(BF16)
    k_second = jnp.where(lower, kl * emid, kl).astype(BF16)
    qlb = ql.astype(BF16)
    p0 = lax.dot_general(qlb[:HALF], k_first, NT_DIMS, preferred_element_type=F32)
    p1 = lax.dot_general(qlb[HALF:], k_second, NT_DIMS, preferred_element_type=F32)
    p = jnp.concatenate([p0, p1], axis=0)
    return jnp.where(row >= col, p, 0.0)


def _scores_exact(z_ref, a_ref, r0, hd, qh, ah, row):
    lane = lax.broadcasted_iota(jnp.int32, (1, HG_DK), 1)
    lo = hd * HG_DK

    def body(g, p):
        src = pl.ds(pl.multiple_of(r0 + g * SUBLANES, SUBLANES), SUBLANES)
        a_g = a_ref[src, lo:lo + HG_DK]
        k_g = z_ref[src, Z_F + lo:Z_F + lo + HG_DK]
        for i in range(SUBLANES):
            w = qh * k_g[i:i + 1, :] * jnp.exp(jnp.minimum(ah - a_g[i:i + 1, :], 0.0))
            p = jnp.where(lane == g * SUBLANES + i, jnp.sum(w, axis=-1, keepdims=True), p)
        return p

    p = lax.fori_loop(0, CHUNK // SUBLANES, body, jnp.zeros((CHUNK, HG_DK), F32))
    return jnp.where(row >= lane, p, 0.0)[:, :CHUNK]


def _layer_kernel(li, pos0, spt, tps, nt, ns, final,
                  xn_ref, xc_ref, gmix_ref, win_ref, lbp_ref, hgg_ref, cw_ref, cb_ref, clg_ref, clb_ref,
                  pw_ref, ps_ref, wout_ref, hg0_ref, conv0_ref, pool0_ref,
                  gx_ref, wq_ref, wo_ref, k_ref, v_ref,
                  gffn_ref, wup_ref, wdn_ref, fg_ref,
                  y_ref, hg_out_ref, conv_out_ref, pool_out_ref,
                  z_ref, a_ref, mix_ref, dp_ref, st_ref, cfull_ref, c8_ref, pfull_ref, s2_ref, s4_ref, s8_ref,
                  p_ref, qa_ref, ks_ref, vb_ref, sog_ref, dec_ref):
    j = pl.program_id(0)
    tn = lax.rem(jnp.minimum(j, ns - 1), nt)
    tp = lax.rem(jnp.maximum(j - 1, 0), nt)
    tt = spt * tps
    cps = tps // CHUNK
    nchunks = spt * cps
    seg_c = CONV_PAD + tps
    seg_p = POOL_PAD + tps
    carried = spt == 1

    @pl.when(j == 0)
    def _():
        for ref in (p_ref, qa_ref, ks_ref, vb_ref, sog_ref, dec_ref, mix_ref, dp_ref, pfull_ref, s2_ref, s4_ref):
            ref[...] = jnp.zeros(ref.shape, ref.dtype)

    def load_state(seg):
        for hd in range(HG_HEADS):
            st_ref[hd] = hg0_ref[seg, hd].T

    def store_state(seg):
        for hd in range(HG_HEADS):
            hg_out_ref[seg, hd] = st_ref[hd].T

    def load_histories(seg):
        cfull_ref[pl.ds(seg * seg_c + CONV_PAD - CONV_HIST, CONV_HIST), :] = conv0_ref[seg]
        pfull_ref[pl.ds(seg * seg_p + POOL_PAD - POOL_HIST, POOL_HIST), :] = pool0_ref[seg]

    def store_histories(seg):
        conv_out_ref[seg] = cfull_ref[pl.ds(seg * seg_c + tps + CONV_PAD - CONV_HIST, CONV_HIST), :]
        pool_out_ref[seg] = pfull_ref[pl.ds(seg * seg_p + tps + POOL_PAD - POOL_HIST, POOL_HIST), :]

    if carried:
        pl.when(tp == 0)(functools.partial(load_state, 0))
        pl.when(tn == 0)(functools.partial(load_histories, 0))
    else:
        for seg in range(spt):
            load_histories(seg)

    hgg = hgg_ref[...]
    for c in range(nchunks):
        rows = pl.ds(c * CHUNK, CHUNK)
        if not carried and c % cps == 0:
            load_state(c // cps)
        for hd in range(HG_HEADS):
            cols = slice(hd * HG_DK, (hd + 1) * HG_DK)
            st = st_ref[hd]
            vb = vb_ref[rows, cols]
            o = (_dot(p_ref[hd, rows, :], vb)
                 + lax.dot_general(qa_ref[rows, cols], st.astype(BF16), NT_DIMS, preferred_element_type=F32))
            st_ref[hd] = dec_ref[c, :, cols] * st + lax.dot_general(vb, ks_ref[rows, cols], TN_DIMS,
                                                                     preferred_element_type=F32)
            mix_ref[rows, cols] = (_rms(o, hgg[:, cols]) * sog_ref[rows, cols]).astype(BF16)
        if not carried and c % cps == cps - 1:
            store_state(c // cps)

    c_out = _dot(dp_ref[...], pw_ref[...]) * ps_ref[...]
    mix_ref[:, HG_WIDTH + CONV_WIDTH:] = c_out.astype(BF16)
    x1 = xc_ref[...].reshape(tt, D_MODEL) + _dot(mix_ref[...], wout_ref[...])

    h = _rms(xn_ref[...].reshape(tt, D_MODEL), gmix_ref[...]).astype(BF16)
    for lo, hi in ((Z_F, Z_V), (Z_CA, Z_PU), (Z_PU, D_IN), (Z_Q, Z_F), (Z_V, Z_OG), (Z_OG, Z_CA)):
        z_ref[:, lo:hi] = _dot(h, win_ref[:, lo:hi])

    hx = _rms(x1, gx_ref[...]).astype(BF16)
    q = (_dot(hx, wq_ref[...]) * (1.0 / 16.0)).astype(BF16)
    seg_outs = []
    for seg in range(spt):
        seg_rows = slice(seg * tps, (seg + 1) * tps)
        outs = []
        for hd in range(X_HEADS):
            cols = slice(hd * X_HDIM, (hd + 1) * X_HDIM)
            sc = lax.dot_general(q[seg_rows, cols], k_ref[seg, :, cols], NT_DIMS, preferred_element_type=F32)
            pexp = jnp.exp(sc - jnp.max(sc, axis=-1, keepdims=True))
            prob = pexp / jnp.sum(pexp, axis=-1, keepdims=True)
            outs.append(_dot(prob.astype(BF16), v_ref[seg, :, cols]).astype(BF16))
        seg_outs.append(jnp.concatenate(outs, axis=1))
    attn = seg_outs[0] if spt == 1 else jnp.concatenate(seg_outs, axis=0)
    x2 = x1 + _dot(attn, wo_ref[...])

    hf = _rms(x2, gffn_ref[...]).astype(BF16)
    ff_blk = D_FF // FF_BLOCKS

    def ffn_block(y, c):
        blk = slice(c * ff_blk, (c + 1) * ff_blk)
        u = jnp.maximum(_dot(hf, wup_ref[:, blk]), 0.0)
        return y + _dot((u * u).astype(BF16), wdn_ref[blk, :])

    y = x2
    for c in range(FF_BLOCKS - 2):
        y = ffn_block(y, c)

    lbp = lbp_ref[...]
    e = jnp.exp(lbp - jnp.max(lbp, axis=0, keepdims=True))
    sm = e / jnp.sum(e, axis=0, keepdims=True)
    lb = jnp.zeros((1, HG_WIDTH), F32)
    for i in range(1, li + 1):
        lb = lb + sm[i:i + 1, :]
    oml = 1.0 - lb

    lane = lax.broadcasted_iota(jnp.int32, (1, 128), 1)
    low_lanes = lane < 64

    for seg in range(spt):
        seg_rows = pl.ds(seg * tps, tps)
        cfull_ref[pl.ds(seg * seg_c + CONV_PAD, tps), :] = (
            z_ref[seg_rows, Z_CA:Z_CG] * jax.nn.sigmoid(z_ref[seg_rows, Z_CG:Z_PU]))
        for b in range(SUBLANES):
            n = tps + (CONV_K - 1 - b) // SUBLANES * SUBLANES
            c8_ref[b, pl.ds(seg * seg_c + CONV_SHIFT_ROWS, n), :] = (
                cfull_ref[pl.ds(seg * seg_c + CONV_PAD - CONV_HIST + b, n), :])

    for c in range(nchunks):
        r0 = (c // cps) * seg_c + (c % cps) * CHUNK + CONV_SHIFT_ROWS
        dw = jnp.broadcast_to(cb_ref[...], (CHUNK, CONV_WIDTH))
        for tap in range(CONV_K):
            a, b = divmod(tap, SUBLANES)
            dw = dw + c8_ref[b, pl.ds(r0 + SUBLANES * a, CHUNK), :] * cw_ref[tap:tap + 1, :]
        mu = jnp.mean(dw, axis=-1, keepdims=True)
        xc = dw - mu
        var = jnp.mean(xc * xc, axis=-1, keepdims=True)
        yb = xc * lax.rsqrt(var + EPS) * clg_ref[...] + clb_ref[...]
        mix_ref[pl.ds(c * CHUNK, CHUNK), HG_WIDTH:HG_WIDTH + CONV_WIDTH] = _silu(yb).astype(BF16)

    for seg in range(spt):
        pfull_ref[pl.ds(seg * seg_p + POOL_PAD, tps), :] = z_ref[pl.ds(seg * tps, tps), Z_PU:D_IN]
    n = spt * seg_p - SUBLANES
    s2_ref[pl.ds(8, n), :] = pfull_ref[pl.ds(8, n), :] + pfull_ref[pl.ds(7, n), :]
    s4_ref[pl.ds(8, n), :] = s2_ref[pl.ds(8, n), :] + s2_ref[pl.ds(6, n), :]
    s8_ref[pl.ds(8, n), :] = s4_ref[pl.ds(8, n), 128:256] + s4_ref[pl.ds(4, n), 128:256]
    posf = (pos0 + 1 + tn * tps + lax.broadcasted_iota(jnp.int32, (tps, 1), 0)).astype(F32)
    cnt_a = jnp.minimum(posf, jnp.where(low_lanes, 2.0, 4.0))
    cnt_b = jnp.minimum(posf, jnp.where(low_lanes, 8.0, 16.0))
    for seg in range(spt):
        cur = pl.ds(seg * seg_p + POOL_PAD, tps)
        s16 = s8_ref[cur, :] + s8_ref[pl.ds(seg * seg_p + POOL_PAD - 8, tps), :]
        win_a = jnp.where(low_lanes, s2_ref[cur, 0:128], s4_ref[cur, 0:128])
        win_b = jnp.where(low_lanes, s8_ref[cur, :], s16)
        seg_rows = pl.ds(seg * tps, tps)
        dp_ref[seg_rows, 0:128] = (win_a / cnt_a - pfull_ref[cur, 0:128]).astype(BF16)
        dp_ref[seg_rows, 128:256] = (win_b / cnt_b - pfull_ref[cur, 128:256]).astype(BF16)

    row = lax.broadcasted_iota(jnp.int32, (CHUNK, 1), 0)
    col = lax.broadcasted_iota(jnp.int32, (1, CHUNK), 1)
    tri = (row >= col).astype(BF16)
    dmax = jnp.zeros((1, HG_WIDTH), F32)
    for c in range(nchunks):
        rows = pl.ds(c * CHUNK, CHUNK)
        fl = z_ref[rows, Z_F:Z_V]
        f_gate = lb + oml * jax.nn.sigmoid(fl)
        log_f = jnp.log(jnp.maximum(f_gate, F_FLOOR))
        hi = log_f.astype(BF16)
        lo = (log_f - hi.astype(F32)).astype(BF16)
        cs = _dot(tri, jnp.concatenate([hi, lo], axis=1))
        a = cs[:, :HG_WIDTH] + cs[:, HG_WIDTH:]
        k = oml * jax.nn.sigmoid(-fl)
        z_ref[rows, Z_F:Z_V] = k
        a_ref[rows, :] = a
        amid = a[HALF - 1:HALF, :]
        aend = a[CHUNK - 1:CHUNK, :]
        dmax = jnp.maximum(dmax, jnp.maximum(-amid, amid - aend))
    unsafe = jnp.max(dmax) > SAFE_DECAY

    y = ffn_block(y, FF_BLOCKS - 2)

    for c in range(nchunks):
        rows = pl.ds(c * CHUNK, CHUNK)
        a = a_ref[rows, :]
        k = z_ref[rows, Z_F:Z_V]
        aend = a[CHUNK - 1:CHUNK, :]
        qs = _silu(z_ref[rows, Z_Q:Z_F])
        qa_ref[rows, :] = (qs * jnp.exp(a)).astype(BF16)
        ks_ref[rows, :] = (k * jnp.exp(aend - a)).astype(BF16)
        dec_ref[c] = jnp.exp(aend)
        vb_ref[rows, :] = z_ref[rows, Z_V:Z_OG].astype(BF16)
        sog_ref[rows, :] = _silu(z_ref[rows, Z_OG:Z_CA])
        for hd in range(HG_HEADS):
            cols = slice(hd * HG_DK, (hd + 1) * HG_DK)
            p_ref[hd, rows, :] = _scores_factored(qs[:, cols], k[:, cols], a[:, cols], row, col).astype(BF16)

    y = ffn_block(y, FF_BLOCKS - 1)
    y_ref[...] = (_rms(y, fg_ref[...]) if final else y).reshape(spt, tps, D_MODEL)

    @pl.when(unsafe)
    def _():
        def chunk(c, carry):
            r0 = pl.multiple_of(c * CHUNK, CHUNK)
            rows = pl.ds(r0, CHUNK)
            for hd in range(HG_HEADS):
                lo = hd * HG_DK
                qh = _silu(z_ref[rows, Z_Q + lo:Z_Q + lo + HG_DK])
                ah = a_ref[rows, lo:lo + HG_DK]
                p_ref[hd, rows, :] = _scores_exact(z_ref, a_ref, r0, hd, qh, ah, row).astype(BF16)
            return carry

        lax.fori_loop(0, nchunks, chunk, 0)

    if carried:
        pl.when((tp == nt - 1) & (j >= 1))(functools.partial(store_state, 0))
        pl.when((tn == nt - 1) & (j < ns))(functools.partial(store_histories, 0))
        cfull_ref[pl.ds(CONV_PAD - CONV_HIST, CONV_HIST), :] = (
            cfull_ref[pl.ds(tps + CONV_PAD - CONV_HIST, CONV_HIST), :])
        pfull_ref[pl.ds(POOL_PAD - POOL_HIST, POOL_HIST), :] = (
            pfull_ref[pl.ds(tps + POOL_PAD - POOL_HIST, POOL_HIST), :])
    else:
        @pl.when(j < ns)
        def _():
            for seg in range(spt):
                store_histories(seg)


def _layer(li, pos0, x, mem_k, mem_v, hg0, conv0, pool0, p):
    b, t, _ = x.shape
    tps = min(t, TOKEN_TILE)
    spt = TOKEN_TILE // tps
    assert t % tps == 0 and tps % CHUNK == 0 and b % spt == 0 and (spt == 1 or t == tps)
    tt = spt * tps
    nt = t // tps
    ns = (b // spt) * nt
    nchunks = tt // CHUNK
    kern = functools.partial(_layer_kernel, li, pos0, spt, tps, nt, ns, li == DEPTH - 1)

    front = lambda j: jnp.minimum(j, ns - 1)
    back = lambda j: jnp.maximum(j - 1, 0)

    def resident(shape):
        zeros = (0,) * len(shape)
        return pl.BlockSpec((None,) + tuple(shape), lambda j: (li,) + zeros, pipeline_mode=pl.Buffered(1))

    stream_mode = dict(pipeline_mode=pl.Buffered(1)) if spt > 1 else {}

    def per_stream(shape, tile, **mode):
        zeros = (0,) * len(shape)
        return pl.BlockSpec((spt,) + tuple(shape), lambda j: (tile(j) // nt,) + zeros, **mode)

    def tile_spec(tile):
        return pl.BlockSpec((spt, tps, D_MODEL), lambda j: (tile(j) // nt, tile(j) % nt, 0))

    kv_spec = pl.BlockSpec((None, spt, N_MEM, D_MODEL), lambda j: (li, back(j) // nt, 0, 0), **stream_mode)
    seg_c, seg_p = CONV_PAD + tps, POOL_PAD + tps
    return pl.pallas_call(
        kern,
        grid=(ns + 1,),
        in_specs=[
            tile_spec(front), tile_spec(back),
            resident((1, D_MODEL)),
            resident((D_MODEL, D_IN)),
            pl.BlockSpec((DEPTH, HG_WIDTH), lambda j: (0, 0)),
            resident((1, HG_WIDTH)),
            resident((CONV_K, CONV_WIDTH)),
            resident((1, CONV_WIDTH)),
            resident((1, CONV_WIDTH)),
            resident((1, CONV_WIDTH)),
            resident((POOL_WIDTH, POOL_WIDTH)),
            resident((1, POOL_WIDTH)),
            resident((D_MODEL, D_MODEL)),
            per_stream((HG_HEADS, HG_DK, HG_DK), back, **stream_mode),
            per_stream((CONV_HIST, CONV_WIDTH), front),
            per_stream((POOL_HIST, POOL_WIDTH), front),
            resident((1, D_MODEL)),
            resident((D_MODEL, D_MODEL)),
            resident((D_MODEL, D_MODEL)),
            kv_spec, kv_spec,
            resident((1, D_MODEL)),
            resident((D_MODEL, D_FF)),
            resident((D_FF, D_MODEL)),
            pl.BlockSpec((1, D_MODEL), lambda j: (0, 0)),
        ],
        out_specs=[
            tile_spec(back),
            per_stream((HG_HEADS, HG_DK, HG_DK), back),
            per_stream((CONV_HIST, CONV_WIDTH), front),
            per_stream((POOL_HIST, POOL_WIDTH), front),
        ],
        out_shape=[
            jax.ShapeDtypeStruct(x.shape, F32),
            jax.ShapeDtypeStruct((b, HG_HEADS, HG_DK, HG_DK), F32),
            jax.ShapeDtypeStruct((b, CONV_HIST, CONV_WIDTH), F32),
            jax.ShapeDtypeStruct((b, POOL_HIST, POOL_WIDTH), F32),
        ],
        scratch_shapes=[
            pltpu.VMEM((tt, D_IN), F32),
            pltpu.VMEM((tt, HG_WIDTH), F32),
            pltpu.VMEM((tt, D_MODEL), BF16),
            pltpu.VMEM((tt, POOL_WIDTH), BF16),
            pltpu.VMEM((HG_HEADS, HG_DK, HG_DK), F32),
            pltpu.VMEM((spt * seg_c, CONV_WIDTH), F32),
            pltpu.VMEM((SUBLANES, spt * seg_c, CONV_WIDTH), F32),
            pltpu.VMEM((spt * seg_p, POOL_WIDTH), F32),
            pltpu.VMEM((spt * seg_p, POOL_WIDTH), F32),
            pltpu.VMEM((spt * seg_p, POOL_WIDTH), F32),
            pltpu.VMEM((spt * seg_p, 128), F32),
            pltpu.VMEM((HG_HEADS, tt, CHUNK), BF16),
            pltpu.VMEM((tt, HG_WIDTH), BF16),
            pltpu.VMEM((tt, HG_WIDTH), BF16),
            pltpu.VMEM((tt, HG_WIDTH), BF16),
            pltpu.VMEM((tt, HG_WIDTH), F32),
            pltpu.VMEM((nchunks, 1, HG_WIDTH), F32),
        ],
        compiler_params=pltpu.CompilerParams(
            dimension_semantics=("arbitrary",), vmem_limit_bytes=VMEM_LIMIT_V7X),
        name=f"layer{li}",
    )(x, x, p["norm_mix_g"], p["w_in"], p["lb_param"], p["hg_norm_g"], p["conv_w"], p["conv_b"],
      p["conv_ln_g"], p["conv_ln_b"], p["pool_w_bd"], p["pool_scale"], p["w_out"], hg0, conv0, pool0,
      p["norm_x_g"], p["xq_w"], p["xo_w"], mem_k, mem_v,
      p["norm_ffn_g"], p["w_up"], p["w_down"], p["final_g"])


def _memkv_kernel(m_ref, g_ref, wk_ref, wv_ref, k_ref, v_ref, kb_ref, vb_ref):
    h = _rms(m_ref[...], g_ref[...]).astype(BF16)
    k = _dot(h, wk_ref[...])
    v = _dot(h, wv_ref[...])
    k_ref[...] = k
    v_ref[...] = v
    kb_ref[...] = k.astype(BF16)
    vb_ref[...] = v.astype(BF16)


def _memkv(mem, p):
    b = mem.shape[0]
    n = b * N_MEM
    rt = 1024
    out_spec = pl.BlockSpec((None, rt, D_MODEL), lambda l, r: (l, r, 0))
    w_spec = pl.BlockSpec((None, D_MODEL, D_MODEL), lambda l, r: (l, 0, 0))
    return pl.pallas_call(
        _memkv_kernel,
        grid=(DEPTH, n // rt),
        in_specs=[
            pl.BlockSpec((rt, D_MODEL), lambda l, r: (r, 0)),
            pl.BlockSpec((None, 1, D_MODEL), lambda l, r: (l, 0, 0)),
            w_spec, w_spec,
        ],
        out_specs=[out_spec] * 4,
        out_shape=[jax.ShapeDtypeStruct((DEPTH, n, D_MODEL), F32)] * 2
        + [jax.ShapeDtypeStruct((DEPTH, n, D_MODEL), BF16)] * 2,
        compiler_params=pltpu.CompilerParams(
            dimension_semantics=("arbitrary", "arbitrary"), vmem_limit_bytes=VMEM_LIMIT_V7X),
        name="memkv",
    )(mem.reshape(n, D_MODEL), p["norm_mem_g"], p["xk_w"], p["xv_w"])


def _trunk(x, mem_k, mem_v, hg_state, conv_buf, pool_buf, pos0, p):
    new_hg, new_conv, new_pool = [], [], []
    for li in range(DEPTH):
        x, s_hg, s_conv, s_pool = _layer(li, pos0, x, mem_k, mem_v, hg_state[li], conv_buf[li], pool_buf[li], p)
        new_hg.append(s_hg)
        new_conv.append(s_conv)
        new_pool.append(s_pool)
    return x, jnp.stack(new_hg), jnp.stack(new_conv), jnp.stack(new_pool)


def kernel(x_prompt, x_sample, mem_prompt, state_hgrn, cache_conv, cache_pool, cache_mem_k, cache_mem_v,
           norm_mix_g, w_in, lb_param, hg_norm_g, conv_w, conv_b, conv_ln_g, conv_ln_b, pool_w, pool_scale,
           w_out, norm_x_g, norm_mem_g, xq_w, xk_w, xv_w, xo_w, norm_ffn_g, w_up, w_down, final_g):
    row = lambda a: a.reshape(a.shape[0], 1, a.shape[1])
    groups = pool_w.shape[1]
    pool_w_bd = (pool_w[:, :, :, None, :] * jnp.eye(groups, dtype=pool_w.dtype)[None, :, None, :, None]
                 ).reshape(DEPTH, POOL_WIDTH, POOL_WIDTH)
    p = dict(
        norm_mix_g=row(norm_mix_g), w_in=w_in.astype(BF16), lb_param=lb_param, hg_norm_g=row(hg_norm_g),
        conv_w=conv_w, conv_b=row(conv_b), conv_ln_g=row(conv_ln_g), conv_ln_b=row(conv_ln_b),
        pool_w_bd=pool_w_bd.astype(BF16), pool_scale=row(pool_scale), w_out=w_out.astype(BF16),
        norm_x_g=row(norm_x_g), norm_mem_g=row(norm_mem_g), xq_w=xq_w.astype(BF16), xk_w=xk_w.astype(BF16),
        xv_w=xv_w.astype(BF16), xo_w=xo_w.astype(BF16), norm_ffn_g=row(norm_ffn_g),
        w_up=w_up.astype(BF16), w_down=w_down.astype(BF16),
        final_g=final_g.reshape(1, D_MODEL))

    bp = x_prompt.shape[0]
    mem_k_p, mem_v_p, mem_kb, mem_vb = _memkv(mem_prompt, p)
    kv_shape = (DEPTH, bp, N_MEM, X_HEADS, X_HDIM)
    flat_kv = (DEPTH, bp, N_MEM, D_MODEL)

    dt = x_prompt.dtype
    hg0 = jnp.zeros((DEPTH, bp, HG_HEADS, HG_DK, HG_DK), dt)
    conv0 = jnp.zeros((DEPTH, bp, CONV_HIST, CONV_WIDTH), dt)
    pool0 = jnp.zeros((DEPTH, bp, POOL_HIST, POOL_WIDTH), dt)
    y_prompt, hg_p, conv_p, pool_p = _trunk(
        x_prompt, mem_kb.reshape(flat_kv), mem_vb.reshape(flat_kv), hg0, conv0, pool0, 0, p)

    bs = x_sample.shape[0]
    y_sample, hg_s, conv_s, pool_s = _trunk(
        x_sample, cache_mem_k.reshape(DEPTH, bs, N_MEM, D_MODEL).astype(BF16),
        cache_mem_v.reshape(DEPTH, bs, N_MEM, D_MODEL).astype(BF16),
        state_hgrn, cache_conv, cache_pool, PAST_LEN, p)

    return (y_prompt, y_sample, hg_p, conv_p, pool_p, mem_k_p.reshape(kv_shape), mem_v_p.reshape(kv_shape),
            hg_s, conv_s, pool_s)
```

```python
import functools

import jax
import jax.numpy as jnp
from jax import lax
from jax.experimental import pallas as pl
from jax.experimental.pallas import tpu as pltpu

F32 = jnp.float32
BF16 = jnp.bfloat16

D_MODEL = 1024
DEPTH = 4
CHUNK = 64
HALF = CHUNK // 2
N_MEM = 256
HG_DK = 128
HG_WIDTH = 512
HG_HEADS = 4
CONV_WIDTH = 256
CONV_K = 31
CONV_HIST = CONV_K - 1
POOL_WIDTH = 256
POOL_HIST = 15
D_IN = 4 * HG_WIDTH + 2 * CONV_WIDTH + POOL_WIDTH
X_HEADS = 4
X_HDIM = 256
D_FF = 4096
EPS = 1e-6
F_FLOOR = 1e-30
PAST_LEN = 2048

Z_Q, Z_F, Z_V, Z_OG, Z_CA, Z_CG, Z_PU = 0, 512, 1024, 1536, 2048, 2304, 2560

SUBLANES = 8
CONV_PAD = 32
POOL_PAD = 24
CONV_SHIFT_ROWS = 8

SAFE_DECAY = 80.0

TOKEN_TILE = 256
FF_BLOCKS = 4
LANES = 128
LANE_PAD = LANES
VMEM_LIMIT_V7X = 58 * 1024 * 1024

NT_DIMS = (((1,), (1,)), ((), ()))
TN_DIMS = (((0,), (0,)), ((), ()))


def _rms(x, g):
    ms = jnp.mean(x * x, axis=-1, keepdims=True)
    return x * lax.rsqrt(ms + EPS) * g


def _silu(x):
    return x * jax.nn.sigmoid(x)


def _dot(a, b):
    return jnp.dot(a, b, preferred_element_type=F32)


def _scores_factored(qh, kh, ah, row, col):
    lower = row < HALF
    amid = ah[HALF - 1:HALF, :]
    ref = jnp.where(lower, 0.0, amid)
    ql = qh * jnp.exp(ah - ref)
    kl = kh * jnp.exp(ref - ah)
    emid = jnp.exp(amid)
    k_first = jnp.where(lower, kl, 0.0).astype(BF16)
    k_second = jnp.where(lower, kl * emid, kl).astype(BF16)
    qlb = ql.astype(BF16)
    p0 = lax.dot_general(qlb[:HALF], k_first, NT_DIMS, preferred_element_type=F32)
    p1 = lax.dot_general(qlb[HALF:], k_second, NT_DIMS, preferred_element_type=F32)
    p = jnp.concatenate([p0, p1], axis=0)
    return jnp.where(row >= col, p, 0.0)


def _scores_exact(z_ref, a_ref, r0, hd, qh, ah, row):
    lane = lax.broadcasted_iota(jnp.int32, (1, HG_DK), 1)
    lo = hd * HG_DK

    def body(g, p):
        src = pl.ds(pl.multiple_of(r0 + g * SUBLANES, SUBLANES), SUBLANES)
        a_g = a_ref[src, lo:lo + HG_DK]
        k_g = z_ref[src, Z_F + lo:Z_F + lo + HG_DK]
        for i in range(SUBLANES):
            w = qh * k_g[i:i + 1, :] * jnp.exp(jnp.minimum(ah - a_g[i:i + 1, :], 0.0))
            p = jnp.where(lane == g * SUBLANES + i, jnp.sum(w, axis=-1, keepdims=True), p)
        return p

    p = lax.fori_loop(0, CHUNK // SUBLANES, body, jnp.zeros((CHUNK, HG_DK), F32))
    return jnp.where(row >= lane, p, 0.0)[:, :CHUNK]


def _layer_kernel(li, pos0, spt, tps, nt, ns, final,
                  xn_ref, xc_ref, gmix_ref, win_ref, lbp_ref, hgg_ref, cw_ref, cb_ref, clg_ref, clb_ref,
                  pw_ref, ps_ref, wout_ref, hg0_ref, conv0_ref, pool0_ref,
                  gx_ref, wq_ref, wo_ref, k_ref, v_ref,
                  gffn_ref, wup_ref, wdn_ref, fg_ref,
                  y_ref, hg_out_ref, conv_out_ref, pool_out_ref,
                  z_ref, a_ref, mix_ref, dp_ref, st_ref, cfull_ref, c8_ref, pfull_ref, s2_ref, s4_ref, s8_ref,
                  p_ref, qa_ref, ks_ref, vb_ref, sog_ref, dec_ref):
    j = pl.program_id(0)
    tn = lax.rem(jnp.minimum(j, ns - 1), nt)
    tp = lax.rem(jnp.maximum(j - 1, 0), nt)
    tt = spt * tps
    cps = tps // CHUNK
    nchunks = spt * cps
    seg_c = CONV_PAD + tps
    seg_p = POOL_PAD + tps
    carried = spt == 1

    @pl.when(j == 0)
    def _():
        for ref in (p_ref, qa_ref, ks_ref, vb_ref, sog_ref, dec_ref, mix_ref, dp_ref, pfull_ref, s2_ref, s4_ref):
            ref[...] = jnp.zeros(ref.shape, ref.dtype)

    def load_state(seg):
        for hd in range(HG_HEADS):
            st_ref[hd] = hg0_ref[seg, hd].T

    def store_state(seg):
        for hd in range(HG_HEADS):
            hg_out_ref[seg, hd] = st_ref[hd].T

    def load_histories(seg):
        cfull_ref[pl.ds(seg * seg_c + CONV_PAD - CONV_HIST, CONV_HIST), :] = conv0_ref[seg]
        pfull_ref[pl.ds(seg * seg_p + POOL_PAD - POOL_HIST, POOL_HIST), :] = pool0_ref[seg]

    def store_histories(seg):
        conv_out_ref[seg] = cfull_ref[pl.ds(seg * seg_c + tps + CONV_PAD - CONV_HIST, CONV_HIST), :]
        pool_out_ref[seg] = pfull_ref[pl.ds(seg * seg_p + tps + POOL_PAD - POOL_HIST, POOL_HIST), :]

    if carried:
        pl.when(tp == 0)(functools.partial(load_state, 0))
        pl.when(tn == 0)(functools.partial(load_histories, 0))
    else:
        for seg in range(spt):
            load_histories(seg)

    hgg = hgg_ref[...]
    for c in range(nchunks):
        rows = pl.ds(c * CHUNK, CHUNK)
        if not carried and c % cps == 0:
            load_state(c // cps)
        for hd in range(HG_HEADS):
            cols = slice(hd * HG_DK, (hd + 1) * HG_DK)
            st = st_ref[hd]
            vb = vb_ref[rows, cols]
            o = (_dot(p_ref[hd, rows, :], vb)
                 + lax.dot_general(qa_ref[rows, cols], st.astype(BF16), NT_DIMS, preferred_element_type=F32))
            st_ref[hd] = dec_ref[c, :, cols] * st + lax.dot_general(vb, ks_ref[rows, cols], TN_DIMS,
                                                                     preferred_element_type=F32)
            mix_ref[rows, cols] = (_rms(o, hgg[:, cols]) * sog_ref[rows, cols]).astype(BF16)
        if not carried and c % cps == cps - 1:
            store_state(c // cps)

    c_out = _dot(dp_ref[...], pw_ref[...]) * ps_ref[...]
    mix_ref[:, HG_WIDTH + CONV_WIDTH:] = c_out.astype(BF16)
    x1 = xc_ref[...].reshape(tt, D_MODEL) + _dot(mix_ref[...], wout_ref[:, :D_MODEL])

    h = _rms(xn_ref[...].reshape(tt, D_MODEL), gmix_ref[...]).astype(BF16)
    for lo, hi in ((Z_F, Z_V), (Z_CA, Z_PU), (Z_PU, D_IN), (Z_Q, Z_F), (Z_V, Z_OG), (Z_OG, Z_CA)):
        z_ref[:, lo:hi] = _dot(h, win_ref[:, lo:hi])

    hx = _rms(x1, gx_ref[...]).astype(BF16)
    q = (_dot(hx, wq_ref[:, :D_MODEL]) * (1.0 / 16.0)).astype(BF16)
    heads = [(seg, hd) for seg in range(spt) for hd in range(X_HEADS)]
    scores = {}
    for seg, hd in heads:
        cols = slice(hd * X_HDIM, (hd + 1) * X_HDIM)
        scores[seg, hd] = lax.dot_general(q[seg * tps:(seg + 1) * tps, cols], k_ref[seg, :, cols], NT_DIMS,
                                          preferred_element_type=F32)
    seg_outs = []
    for seg in range(spt):
        outs = []
        for hd in range(X_HEADS):
            cols = slice(hd * X_HDIM, (hd + 1) * X_HDIM)
            sc = scores[seg, hd]
            pexp = jnp.exp(sc - jnp.max(sc, axis=-1, keepdims=True))
            prob = pexp / jnp.sum(pexp, axis=-1, keepdims=True)
            outs.append(_dot(prob.astype(BF16), v_ref[seg, :, cols]).astype(BF16))
        seg_outs.append(jnp.concatenate(outs, axis=1))
    attn = seg_outs[0] if spt == 1 else jnp.concatenate(seg_outs, axis=0)
    x2 = x1 + _dot(attn, wo_ref[:, :D_MODEL])

    hf = _rms(x2, gffn_ref[...]).astype(BF16)
    ff_blk = D_FF // FF_BLOCKS

    def ffn_block(y, c):
        blk = slice(c * ff_blk, (c + 1) * ff_blk)
        u = jnp.maximum(_dot(hf, wup_ref[:, blk]), 0.0)
        return y + _dot((u * u).astype(BF16), wdn_ref[blk, :D_MODEL])

    y = x2
    for c in range(FF_BLOCKS - 2):
        y = ffn_block(y, c)

    lbp = lbp_ref[...]
    e = jnp.exp(lbp - jnp.max(lbp, axis=0, keepdims=True))
    sm = e / jnp.sum(e, axis=0, keepdims=True)
    lb = jnp.zeros((1, HG_WIDTH), F32)
    for i in range(1, li + 1):
        lb = lb + sm[i:i + 1, :]
    oml = 1.0 - lb

    lane = lax.broadcasted_iota(jnp.int32, (1, 128), 1)
    low_lanes = lane < 64

    for seg in range(spt):
        seg_rows = pl.ds(seg * tps, tps)
        cfull_ref[pl.ds(seg * seg_c + CONV_PAD, tps), :] = (
            z_ref[seg_rows, Z_CA:Z_CG] * jax.nn.sigmoid(z_ref[seg_rows, Z_CG:Z_PU]))
        for b in range(SUBLANES):
            n = tps + (CONV_K - 1 - b) // SUBLANES * SUBLANES
            c8_ref[b, pl.ds(seg * seg_c + CONV_SHIFT_ROWS, n), :] = (
                cfull_ref[pl.ds(seg * seg_c + CONV_PAD - CONV_HIST + b, n), :])

    for c in range(nchunks):
        r0 = (c // cps) * seg_c + (c % cps) * CHUNK + CONV_SHIFT_ROWS
        dw = jnp.broadcast_to(cb_ref[...], (CHUNK, CONV_WIDTH))
        for tap in range(CONV_K):
            a, b = divmod(tap, SUBLANES)
            dw = dw + c8_ref[b, pl.ds(r0 + SUBLANES * a, CHUNK), :] * cw_ref[tap:tap + 1, :]
        mu = jnp.mean(dw, axis=-1, keepdims=True)
        xc = dw - mu
        var = jnp.mean(xc * xc, axis=-1, keepdims=True)
        yb = xc * lax.rsqrt(var + EPS) * clg_ref[...] + clb_ref[...]
        mix_ref[pl.ds(c * CHUNK, CHUNK), HG_WIDTH:HG_WIDTH + CONV_WIDTH] = _silu(yb).astype(BF16)

    for seg in range(spt):
        pfull_ref[pl.ds(seg * seg_p + POOL_PAD, tps), :] = z_ref[pl.ds(seg * tps, tps), Z_PU:D_IN]
    n = spt * seg_p - SUBLANES
    s2_ref[pl.ds(8, n), :] = pfull_ref[pl.ds(8, n), :] + pfull_ref[pl.ds(7, n), :]
    s4_ref[pl.ds(8, n), :] = s2_ref[pl.ds(8, n), :] + s2_ref[pl.ds(6, n), :]
    s8_ref[pl.ds(8, n), :] = s4_ref[pl.ds(8, n), 128:256] + s4_ref[pl.ds(4, n), 128:256]
    posf = (pos0 + 1 + tn * tps + lax.broadcasted_iota(jnp.int32, (tps, 1), 0)).astype(F32)
    cnt_a = jnp.minimum(posf, jnp.where(low_lanes, 2.0, 4.0))
    cnt_b = jnp.minimum(posf, jnp.where(low_lanes, 8.0, 16.0))
    for seg in range(spt):
        cur = pl.ds(seg * seg_p + POOL_PAD, tps)
        s16 = s8_ref[cur, :] + s8_ref[pl.ds(seg * seg_p + POOL_PAD - 8, tps), :]
        win_a = jnp.where(low_lanes, s2_ref[cur, 0:128], s4_ref[cur, 0:128])
        win_b = jnp.where(low_lanes, s8_ref[cur, :], s16)
        seg_rows = pl.ds(seg * tps, tps)
        dp_ref[seg_rows, 0:128] = (win_a / cnt_a - pfull_ref[cur, 0:128]).astype(BF16)
        dp_ref[seg_rows, 128:256] = (win_b / cnt_b - pfull_ref[cur, 128:256]).astype(BF16)

    row = lax.broadcasted_iota(jnp.int32, (CHUNK, 1), 0)
    col = lax.broadcasted_iota(jnp.int32, (1, CHUNK), 1)
    tri = (row >= col).astype(BF16)
    dmax = jnp.zeros((1, HG_WIDTH), F32)
    for c in range(nchunks):
        rows = pl.ds(c * CHUNK, CHUNK)
        fl = z_ref[rows, Z_F:Z_V]
        f_gate = lb + oml * jax.nn.sigmoid(fl)
        log_f = jnp.log(jnp.maximum(f_gate, F_FLOOR))
        hi = log_f.astype(BF16)
        lo = (log_f - hi.astype(F32)).astype(BF16)
        cs = _dot(tri, jnp.concatenate([hi, lo], axis=1))
        a = cs[:, :HG_WIDTH] + cs[:, HG_WIDTH:]
        k = oml * jax.nn.sigmoid(-fl)
        z_ref[rows, Z_F:Z_V] = k
        a_ref[rows, :] = a
        amid = a[HALF - 1:HALF, :]
        aend = a[CHUNK - 1:CHUNK, :]
        dmax = jnp.maximum(dmax, jnp.maximum(-amid, amid - aend))
    unsafe = jnp.max(dmax) > SAFE_DECAY

    y = ffn_block(y, FF_BLOCKS - 2)

    for c in range(nchunks):
        rows = pl.ds(c * CHUNK, CHUNK)
        a = a_ref[rows, :]
        k = z_ref[rows, Z_F:Z_V]
        aend = a[CHUNK - 1:CHUNK, :]
        qs = _silu(z_ref[rows, Z_Q:Z_F])
        qa_ref[rows, :] = (qs * jnp.exp(a)).astype(BF16)
        ks_ref[rows, :] = (k * jnp.exp(aend - a)).astype(BF16)
        dec_ref[c] = jnp.exp(aend)
        vb_ref[rows, :] = z_ref[rows, Z_V:Z_OG].astype(BF16)
        sog_ref[rows, :] = _silu(z_ref[rows, Z_OG:Z_CA])
        for hd in range(HG_HEADS):
            cols = slice(hd * HG_DK, (hd + 1) * HG_DK)
            p_ref[hd, rows, :] = _scores_factored(qs[:, cols], k[:, cols], a[:, cols], row, col).astype(BF16)

    y = ffn_block(y, FF_BLOCKS - 1)
    y_ref[...] = (_rms(y, fg_ref[...]) if final else y).reshape(spt, tps, D_MODEL)

    @pl.when(unsafe)
    def _():
        def chunk(c, carry):
            r0 = pl.multiple_of(c * CHUNK, CHUNK)
            rows = pl.ds(r0, CHUNK)
            for hd in range(HG_HEADS):
                lo = hd * HG_DK
                qh = _silu(z_ref[rows, Z_Q + lo:Z_Q + lo + HG_DK])
                ah = a_ref[rows, lo:lo + HG_DK]
                p_ref[hd, rows, :] = _scores_exact(z_ref, a_ref, r0, hd, qh, ah, row).astype(BF16)
            return carry

        lax.fori_loop(0, nchunks, chunk, 0)

    if carried:
        pl.when((tp == nt - 1) & (j >= 1))(functools.partial(store_state, 0))
        pl.when((tn == nt - 1) & (j < ns))(functools.partial(store_histories, 0))
        cfull_ref[pl.ds(CONV_PAD - CONV_HIST, CONV_HIST), :] = (
            cfull_ref[pl.ds(tps + CONV_PAD - CONV_HIST, CONV_HIST), :])
        pfull_ref[pl.ds(POOL_PAD - POOL_HIST, POOL_HIST), :] = (
            pfull_ref[pl.ds(tps + POOL_PAD - POOL_HIST, POOL_HIST), :])
    else:
        @pl.when(j < ns)
        def _():
            for seg in range(spt):
                store_histories(seg)


def _layer(li, pos0, x, mem_k, mem_v, hg0, conv0, pool0, p):
    b, t, _ = x.shape
    tps = min(t, TOKEN_TILE)
    spt = TOKEN_TILE // tps
    assert t % tps == 0 and tps % CHUNK == 0 and b % spt == 0 and (spt == 1 or t == tps)
    tt = spt * tps
    nt = t // tps
    ns = (b // spt) * nt
    nchunks = tt // CHUNK
    kern = functools.partial(_layer_kernel, li, pos0, spt, tps, nt, ns, li == DEPTH - 1)

    front = lambda j: jnp.minimum(j, ns - 1)
    back = lambda j: jnp.maximum(j - 1, 0)

    def resident(shape):
        zeros = (0,) * len(shape)
        return pl.BlockSpec((None,) + tuple(shape), lambda j: (li,) + zeros, pipeline_mode=pl.Buffered(1))

    stream_mode = dict(pipeline_mode=pl.Buffered(1)) if spt > 1 else {}

    def per_stream(shape, tile, **mode):
        zeros = (0,) * len(shape)
        return pl.BlockSpec((spt,) + tuple(shape), lambda j: (tile(j) // nt,) + zeros, **mode)

    def tile_spec(tile):
        return pl.BlockSpec((spt, tps, D_MODEL), lambda j: (tile(j) // nt, tile(j) % nt, 0))

    kv_spec = pl.BlockSpec((None, spt, N_MEM, D_MODEL), lambda j: (li, back(j) // nt, 0, 0), **stream_mode)
    seg_c, seg_p = CONV_PAD + tps, POOL_PAD + tps
    return pl.pallas_call(
        kern,
        grid=(ns + 1,),
        in_specs=[
            tile_spec(front), tile_spec(back),
            resident((1, D_MODEL)),
            resident((D_MODEL, D_IN)),
            pl.BlockSpec((DEPTH, HG_WIDTH), lambda j: (0, 0)),
            resident((1, HG_WIDTH)),
            resident((CONV_K, CONV_WIDTH)),
            resident((1, CONV_WIDTH)),
            resident((1, CONV_WIDTH)),
            resident((1, CONV_WIDTH)),
            resident((POOL_WIDTH, POOL_WIDTH)),
            resident((1, POOL_WIDTH)),
            resident((D_MODEL, D_MODEL + LANE_PAD)),
            per_stream((HG_HEADS, HG_DK, HG_DK), back, **stream_mode),
            per_stream((CONV_HIST, CONV_WIDTH), front),
            per_stream((POOL_HIST, POOL_WIDTH), front),
            resident((1, D_MODEL)),
            resident((D_MODEL, D_MODEL + LANE_PAD)),
            resident((D_MODEL, D_MODEL + LANE_PAD)),
            kv_spec, kv_spec,
            resident((1, D_MODEL)),
            resident((D_MODEL, D_FF + LANE_PAD)),
            resident((D_FF, D_MODEL + LANE_PAD)),
            pl.BlockSpec((1, D_MODEL), lambda j: (0, 0)),
        ],
        out_specs=[
            tile_spec(back),
            per_stream((HG_HEADS, HG_DK, HG_DK), back),
            per_stream((CONV_HIST, CONV_WIDTH), front),
            per_stream((POOL_HIST, POOL_WIDTH), front),
        ],
        out_shape=[
            jax.ShapeDtypeStruct(x.shape, F32),
            jax.ShapeDtypeStruct((b, HG_HEADS, HG_DK, HG_DK), F32),
            jax.ShapeDtypeStruct((b, CONV_HIST, CONV_WIDTH), F32),
            jax.ShapeDtypeStruct((b, POOL_HIST, POOL_WIDTH), F32),
        ],
        scratch_shapes=[
            pltpu.VMEM((tt, D_IN), F32),
            pltpu.VMEM((tt, HG_WIDTH), F32),
            pltpu.VMEM((tt, D_MODEL), BF16),
            pltpu.VMEM((tt, POOL_WIDTH), BF16),
            pltpu.VMEM((HG_HEADS, HG_DK, HG_DK), F32),
            pltpu.VMEM((spt * seg_c, CONV_WIDTH), F32),
            pltpu.VMEM((SUBLANES, spt * seg_c, CONV_WIDTH), F32),
            pltpu.VMEM((spt * seg_p, POOL_WIDTH), F32),
            pltpu.VMEM((spt * seg_p, POOL_WIDTH), F32),
            pltpu.VMEM((spt * seg_p, POOL_WIDTH), F32),
            pltpu.VMEM((spt * seg_p, 128), F32),
            pltpu.VMEM((HG_HEADS, tt, CHUNK), BF16),
            pltpu.VMEM((tt, HG_WIDTH), BF16),
            pltpu.VMEM((tt, HG_WIDTH), BF16),
            pltpu.VMEM((tt, HG_WIDTH), BF16),
            pltpu.VMEM((tt, HG_WIDTH), F32),
            pltpu.VMEM((nchunks, 1, HG_WIDTH), F32),
        ],
        compiler_params=pltpu.CompilerParams(
            dimension_semantics=("arbitrary",), vmem_limit_bytes=VMEM_LIMIT_V7X),
        name=f"layer{li}",
    )(x, x, p["norm_mix_g"], p["w_in"], p["lb_param"], p["hg_norm_g"], p["conv_w"], p["conv_b"],
      p["conv_ln_g"], p["conv_ln_b"], p["pool_w_bd"], p["pool_scale"], p["w_out"], hg0, conv0, pool0,
      p["norm_x_g"], p["xq_w"], p["xo_w"], mem_k, mem_v,
      p["norm_ffn_g"], p["w_up"], p["w_down"], p["final_g"])


def _memkv_kernel(m_ref, g_ref, wk_ref, wv_ref, k_ref, v_ref, kb_ref, vb_ref):
    h = _rms(m_ref[...], g_ref[...]).astype(BF16)
    k = _dot(h, wk_ref[...])
    v = _dot(h, wv_ref[...])
    k_ref[...] = k
    v_ref[...] = v
    kb_ref[...] = k.astype(BF16)
    vb_ref[...] = v.astype(BF16)


def _memkv(mem, p):
    b = mem.shape[0]
    n = b * N_MEM
    rt = 1024
    out_spec = pl.BlockSpec((None, rt, D_MODEL), lambda l, r: (l, r, 0))
    w_spec = pl.BlockSpec((None, D_MODEL, D_MODEL), lambda l, r: (l, 0, 0))
    return pl.pallas_call(
        _memkv_kernel,
        grid=(DEPTH, n // rt),
        in_specs=[
            pl.BlockSpec((rt, D_MODEL), lambda l, r: (r, 0)),
            pl.BlockSpec((None, 1, D_MODEL), lambda l, r: (l, 0, 0)),
            w_spec, w_spec,
        ],
        out_specs=[out_spec] * 4,
        out_shape=[jax.ShapeDtypeStruct((DEPTH, n, D_MODEL), F32)] * 2
        + [jax.ShapeDtypeStruct((DEPTH, n, D_MODEL), BF16)] * 2,
        compiler_params=pltpu.CompilerParams(
            dimension_semantics=("arbitrary", "arbitrary"), vmem_limit_bytes=VMEM_LIMIT_V7X),
        name="memkv",
    )(mem.reshape(n, D_MODEL), p["norm_mem_g"], p["xk_w"], p["xv_w"])


def _trunk(x, mem_k, mem_v, hg_state, conv_buf, pool_buf, pos0, p):
    new_hg, new_conv, new_pool = [], [], []
    for li in range(DEPTH):
        x, s_hg, s_conv, s_pool = _layer(li, pos0, x, mem_k, mem_v, hg_state[li], conv_buf[li], pool_buf[li], p)
        new_hg.append(s_hg)
        new_conv.append(s_conv)
        new_pool.append(s_pool)
    return x, jnp.stack(new_hg), jnp.stack(new_conv), jnp.stack(new_pool)


def kernel(x_prompt, x_sample, mem_prompt, state_hgrn, cache_conv, cache_pool, cache_mem_k, cache_mem_v,
           norm_mix_g, w_in, lb_param, hg_norm_g, conv_w, conv_b, conv_ln_g, conv_ln_b, pool_w, pool_scale,
           w_out, norm_x_g, norm_mem_g, xq_w, xk_w, xv_w, xo_w, norm_ffn_g, w_up, w_down, final_g):
    row = lambda a: a.reshape(a.shape[0], 1, a.shape[1])
    lane_pad = lambda w: jnp.pad(w.astype(BF16), ((0, 0), (0, 0), (0, LANE_PAD)))
    groups = pool_w.shape[1]
    pool_w_bd = (pool_w[:, :, :, None, :] * jnp.eye(groups, dtype=pool_w.dtype)[None, :, None, :, None]
                 ).reshape(DEPTH, POOL_WIDTH, POOL_WIDTH)
    p = dict(
        norm_mix_g=row(norm_mix_g), w_in=w_in.astype(BF16), lb_param=lb_param, hg_norm_g=row(hg_norm_g),
        conv_w=conv_w, conv_b=row(conv_b), conv_ln_g=row(conv_ln_g), conv_ln_b=row(conv_ln_b),
        pool_w_bd=pool_w_bd.astype(BF16), pool_scale=row(pool_scale), w_out=lane_pad(w_out),
        norm_x_g=row(norm_x_g), norm_mem_g=row(norm_mem_g), xq_w=lane_pad(xq_w), xk_w=xk_w.astype(BF16),
        xv_w=xv_w.astype(BF16), xo_w=lane_pad(xo_w), norm_ffn_g=row(norm_ffn_g),
        w_up=lane_pad(w_up), w_down=lane_pad(w_down),
        final_g=final_g.reshape(1, D_MODEL))

    bp = x_prompt.shape[0]
    mem_k_p, mem_v_p, mem_kb, mem_vb = _memkv(mem_prompt, p)
    kv_shape = (DEPTH, bp, N_MEM, X_HEADS, X_HDIM)
    flat_kv = (DEPTH, bp, N_MEM, D_MODEL)

    dt = x_prompt.dtype
    hg0 = jnp.zeros((DEPTH, bp, HG_HEADS, HG_DK, HG_DK), dt)
    conv0 = jnp.zeros((DEPTH, bp, CONV_HIST, CONV_WIDTH), dt)
    pool0 = jnp.zeros((DEPTH, bp, POOL_HIST, POOL_WIDTH), dt)
    y_prompt, hg_p, conv_p, pool_p = _trunk(
        x_prompt, mem_kb.reshape(flat_kv), mem_vb.reshape(flat_kv), hg0, conv0, pool0, 0, p)

    bs = x_sample.shape[0]
    y_sample, hg_s, conv_s, pool_s = _trunk(
        x_sample, cache_mem_k.astype(BF16).reshape(DEPTH, bs, N_MEM, D_MODEL),
        cache_mem_v.astype(BF16).reshape(DEPTH, bs, N_MEM, D_MODEL),
        state_hgrn, cache_conv, cache_pool, PAST_LEN, p)

    return (y_prompt, y_sample, hg_p, conv_p, pool_p, mem_k_p.reshape(kv_shape), mem_v_p.reshape(kv_shape),
            hg_s, conv_s, pool_s)
```

```python
import functools

import jax
import jax.numpy as jnp
from jax import lax
from jax.experimental import pallas as pl
from jax.experimental.pallas import tpu as pltpu

F32 = jnp.float32
BF16 = jnp.bfloat16

D_MODEL = 1024
DEPTH = 4
CHUNK = 64
HALF = CHUNK // 2
N_MEM = 256
HG_DK = 128
HG_WIDTH = 512
HG_HEADS = 4
CONV_WIDTH = 256
CONV_K = 31
CONV_HIST = CONV_K - 1
POOL_WIDTH = 256
POOL_HIST = 15
D_IN = 4 * HG_WIDTH + 2 * CONV_WIDTH + POOL_WIDTH
X_HEADS = 4
X_HDIM = 256
D_FF = 4096
EPS = 1e-6
F_FLOOR = 1e-30
PAST_LEN = 2048

Z_Q, Z_F, Z_V, Z_OG, Z_CA, Z_CG, Z_PU = 0, 512, 1024, 1536, 2048, 2304, 2560

SUBLANES = 8
CONV_PAD = 32
POOL_PAD = 24
CONV_SHIFT_ROWS = 8

SAFE_DECAY = 80.0

TOKEN_TILE = 256
FF_BLOCKS = 4
VMEM_LIMIT_V7X = 58 * 1024 * 1024

NT_DIMS = (((1,), (1,)), ((), ()))
TN_DIMS = (((0,), (0,)), ((), ()))


def _rms(x, g):
    ms = jnp.mean(x * x, axis=-1, keepdims=True)
    return x * lax.rsqrt(ms + EPS) * g


def _silu(x):
    return x * jax.nn.sigmoid(x)


def _dot(a, b):
    return jnp.dot(a, b, preferred_element_type=F32)


def _scores_factored(qh, kh, ah, row, col):
    lower = row < HALF
    amid = ah[HALF - 1:HALF, :]
    ref = jnp.where(lower, 0.0, amid)
    ql = qh * jnp.exp(ah - ref)
    kl = kh * jnp.exp(ref - ah)
    emid = jnp.exp(amid)
    q2 = jnp.concatenate([jnp.where(lower, ql, 0.0), jnp.where(lower, 0.0, ql)], axis=1).astype(BF16)
    k2 = jnp.concatenate([kl, jnp.where(lower, kl * emid, kl)], axis=1).astype(BF16)
    p = lax.dot_general(q2, k2, NT_DIMS, preferred_element_type=F32)
    return jnp.where(row >= col, p, 0.0)


def _scores_exact(z_ref, a_ref, r0, hd, qh, ah, row):
    lane = lax.broadcasted_iota(jnp.int32, (1, HG_DK), 1)
    lo = hd * HG_DK

    def body(g, p):
        src = pl.ds(pl.multiple_of(r0 + g * SUBLANES, SUBLANES), SUBLANES)
        a_g = a_ref[src, lo:lo + HG_DK]
        k_g = z_ref[src, Z_F + lo:Z_F + lo + HG_DK]
        for i in range(SUBLANES):
            w = qh * k_g[i:i + 1, :] * jnp.exp(jnp.minimum(ah - a_g[i:i + 1, :], 0.0))
            p = jnp.where(lane == g * SUBLANES + i, jnp.sum(w, axis=-1, keepdims=True), p)
        return p

    p = lax.fori_loop(0, CHUNK // SUBLANES, body, jnp.zeros((CHUNK, HG_DK), F32))
    return jnp.where(row >= lane, p, 0.0)[:, :CHUNK]


def _layer_kernel(li, pos0, spt, tps, nt, ns, final,
                  xn_ref, xc_ref, gmix_ref, win_ref, lbp_ref, hgg_ref, cw_ref, cb_ref, clg_ref, clb_ref,
                  pw_ref, ps_ref, wout_ref, hg0_ref, conv0_ref, pool0_ref,
                  gx_ref, wq_ref, wo_ref, k_ref, v_ref,
                  gffn_ref, wup_ref, wdn_ref, fg_ref,
                  y_ref, hg_out_ref, conv_out_ref, pool_out_ref,
                  z_ref, a_ref, mix_ref, dp_ref, st_ref, cfull_ref, c8_ref, pfull_ref, s2_ref, s4_ref, s8_ref,
                  p_ref, qa_ref, ks_ref, vb_ref, sog_ref, dec_ref):
    j = pl.program_id(0)
    tn = lax.rem(jnp.minimum(j, ns - 1), nt)
    tp = lax.rem(jnp.maximum(j - 1, 0), nt)
    tt = spt * tps
    cps = tps // CHUNK
    nchunks = spt * cps
    seg_c = CONV_PAD + tps
    seg_p = POOL_PAD + tps
    carried = spt == 1

    @pl.when(j == 0)
    def _():
        for ref in (p_ref, qa_ref, ks_ref, vb_ref, sog_ref, dec_ref, mix_ref, dp_ref, pfull_ref, s2_ref, s4_ref):
            ref[...] = jnp.zeros(ref.shape, ref.dtype)

    def load_state(seg):
        for hd in range(HG_HEADS):
            st_ref[hd] = hg0_ref[seg, hd].T

    def store_state(seg):
        for hd in range(HG_HEADS):
            hg_out_ref[seg, hd] = st_ref[hd].T

    def load_histories(seg):
        cfull_ref[pl.ds(seg * seg_c + CONV_PAD - CONV_HIST, CONV_HIST), :] = conv0_ref[seg]
        pfull_ref[pl.ds(seg * seg_p + POOL_PAD - POOL_HIST, POOL_HIST), :] = pool0_ref[seg]

    def store_histories(seg):
        conv_out_ref[seg] = cfull_ref[pl.ds(seg * seg_c + tps + CONV_PAD - CONV_HIST, CONV_HIST), :]
        pool_out_ref[seg] = pfull_ref[pl.ds(seg * seg_p + tps + POOL_PAD - POOL_HIST, POOL_HIST), :]

    if carried:
        pl.when(tp == 0)(functools.partial(load_state, 0))
        pl.when(tn == 0)(functools.partial(load_histories, 0))
    else:
        for seg in range(spt):
            load_histories(seg)

    hgg = hgg_ref[...]
    for c in range(nchunks):
        rows = pl.ds(c * CHUNK, CHUNK)
        if not carried and c % cps == 0:
            load_state(c // cps)
        for hd in range(HG_HEADS):
            cols = slice(hd * HG_DK, (hd + 1) * HG_DK)
            st = st_ref[hd]
            vb = vb_ref[rows, cols]
            o = (_dot(p_ref[hd, rows, :], vb)
                 + lax.dot_general(qa_ref[rows, cols], st.astype(BF16), NT_DIMS, preferred_element_type=F32))
            st_ref[hd] = dec_ref[c, :, cols] * st + lax.dot_general(vb, ks_ref[rows, cols], TN_DIMS,
                                                                     preferred_element_type=F32)
            mix_ref[rows, cols] = (_rms(o, hgg[:, cols]) * sog_ref[rows, cols]).astype(BF16)
        if not carried and c % cps == cps - 1:
            store_state(c // cps)

    c_out = _dot(dp_ref[...], pw_ref[...]) * ps_ref[...]
    mix_ref[:, HG_WIDTH + CONV_WIDTH:] = c_out.astype(BF16)
    x1 = xc_ref[...].reshape(tt, D_MODEL) + _dot(mix_ref[...], wout_ref[...])

    h = _rms(xn_ref[...].reshape(tt, D_MODEL), gmix_ref[...]).astype(BF16)
    for lo, hi in ((Z_F, Z_V), (Z_CA, Z_PU), (Z_PU, D_IN), (Z_Q, Z_F), (Z_V, Z_OG), (Z_OG, Z_CA)):
        z_ref[:, lo:hi] = _dot(h, win_ref[:, lo:hi])

    hx = _rms(x1, gx_ref[...]).astype(BF16)
    q = (_dot(hx, wq_ref[...]) * (1.0 / 16.0)).astype(BF16)
    heads = [(seg, hd) for seg in range(spt) for hd in range(X_HEADS)]
    scores = {}
    for seg, hd in heads:
        cols = slice(hd * X_HDIM, (hd + 1) * X_HDIM)
        scores[seg, hd] = lax.dot_general(q[seg * tps:(seg + 1) * tps, cols], k_ref[seg, :, cols], NT_DIMS,
                                          preferred_element_type=F32)
    seg_outs = []
    for seg in range(spt):
        outs = []
        for hd in range(X_HEADS):
            cols = slice(hd * X_HDIM, (hd + 1) * X_HDIM)
            sc = scores[seg, hd]
            pexp = jnp.exp(sc - jnp.max(sc, axis=-1, keepdims=True))
            prob = pexp / jnp.sum(pexp, axis=-1, keepdims=True)
            outs.append(_dot(prob.astype(BF16), v_ref[seg, :, cols]).astype(BF16))
        seg_outs.append(jnp.concatenate(outs, axis=1))
    attn = seg_outs[0] if spt == 1 else jnp.concatenate(seg_outs, axis=0)
    x2 = x1 + _dot(attn, wo_ref[...])

    hf = _rms(x2, gffn_ref[...]).astype(BF16)
    ff_blk = D_FF // FF_BLOCKS

    def ffn_block(y, c):
        blk = slice(c * ff_blk, (c + 1) * ff_blk)
        u = jnp.maximum(_dot(hf, wup_ref[:, blk]), 0.0)
        return y + _dot((u * u).astype(BF16), wdn_ref[blk, :])

    y = x2
    for c in range(FF_BLOCKS - 2):
        y = ffn_block(y, c)

    lbp = lbp_ref[...]
    e = jnp.exp(lbp - jnp.max(lbp, axis=0, keepdims=True))
    sm = e / jnp.sum(e, axis=0, keepdims=True)
    lb = jnp.zeros((1, HG_WIDTH), F32)
    for i in range(1, li + 1):
        lb = lb + sm[i:i + 1, :]
    oml = 1.0 - lb

    lane = lax.broadcasted_iota(jnp.int32, (1, 128), 1)
    low_lanes = lane < 64

    for seg in range(spt):
        seg_rows = pl.ds(seg * tps, tps)
        cfull_ref[pl.ds(seg * seg_c + CONV_PAD, tps), :] = (
            z_ref[seg_rows, Z_CA:Z_CG] * jax.nn.sigmoid(z_ref[seg_rows, Z_CG:Z_PU]))
        for b in range(SUBLANES):
            n = tps + (CONV_K - 1 - b) // SUBLANES * SUBLANES
            c8_ref[b, pl.ds(seg * seg_c + CONV_SHIFT_ROWS, n), :] = (
                cfull_ref[pl.ds(seg * seg_c + CONV_PAD - CONV_HIST + b, n), :])

    for c in range(nchunks):
        r0 = (c // cps) * seg_c + (c % cps) * CHUNK + CONV_SHIFT_ROWS
        dw = jnp.broadcast_to(cb_ref[...], (CHUNK, CONV_WIDTH))
        for tap in range(CONV_K):
            a, b = divmod(tap, SUBLANES)
            dw = dw + c8_ref[b, pl.ds(r0 + SUBLANES * a, CHUNK), :] * cw_ref[tap:tap + 1, :]
        mu = jnp.mean(dw, axis=-1, keepdims=True)
        xc = dw - mu
        var = jnp.mean(xc * xc, axis=-1, keepdims=True)
        yb = xc * lax.rsqrt(var + EPS) * clg_ref[...] + clb_ref[...]
        mix_ref[pl.ds(c * CHUNK, CHUNK), HG_WIDTH:HG_WIDTH + CONV_WIDTH] = _silu(yb).astype(BF16)

    for seg in range(spt):
        pfull_ref[pl.ds(seg * seg_p + POOL_PAD, tps), :] = z_ref[pl.ds(seg * tps, tps), Z_PU:D_IN]
    n = spt * seg_p - SUBLANES
    s2_ref[pl.ds(8, n), :] = pfull_ref[pl.ds(8, n), :] + pfull_ref[pl.ds(7, n), :]
    s4_ref[pl.ds(8, n), :] = s2_ref[pl.ds(8, n), :] + s2_ref[pl.ds(6, n), :]
    s8_ref[pl.ds(8, n), :] = s4_ref[pl.ds(8, n), 128:256] + s4_ref[pl.ds(4, n), 128:256]
    posf = (pos0 + 1 + tn * tps + lax.broadcasted_iota(jnp.int32, (tps, 1), 0)).astype(F32)
    cnt_a = jnp.minimum(posf, jnp.where(low_lanes, 2.0, 4.0))
    cnt_b = jnp.minimum(posf, jnp.where(low_lanes, 8.0, 16.0))
    for seg in range(spt):
        cur = pl.ds(seg * seg_p + POOL_PAD, tps)
        s16 = s8_ref[cur, :] + s8_ref[pl.ds(seg * seg_p + POOL_PAD - 8, tps), :]
        win_a = jnp.where(low_lanes, s2_ref[cur, 0:128], s4_ref[cur, 0:128])
        win_b = jnp.where(low_lanes, s8_ref[cur, :], s16)
        seg_rows = pl.ds(seg * tps, tps)
        dp_ref[seg_rows, 0:128] = (win_a / cnt_a - pfull_ref[cur, 0:128]).astype(BF16)
        dp_ref[seg_rows, 128:256] = (win_b / cnt_b - pfull_ref[cur, 128:256]).astype(BF16)

    row = lax.broadcasted_iota(jnp.int32, (CHUNK, 1), 0)
    col = lax.broadcasted_iota(jnp.int32, (1, CHUNK), 1)
    tri = (row >= col).astype(BF16)
    dmax = jnp.zeros((1, HG_WIDTH), F32)
    for c in range(nchunks):
        rows = pl.ds(c * CHUNK, CHUNK)
        fl = z_ref[rows, Z_F:Z_V]
        f_gate = lb + oml * jax.nn.sigmoid(fl)
        log_f = jnp.log(jnp.maximum(f_gate, F_FLOOR))
        hi = log_f.astype(BF16)
        lo = (log_f - hi.astype(F32)).astype(BF16)
        cs = _dot(tri, jnp.concatenate([hi, lo], axis=1))
        a = cs[:, :HG_WIDTH] + cs[:, HG_WIDTH:]
        k = oml * jax.nn.sigmoid(-fl)
        z_ref[rows, Z_F:Z_V] = k
        a_ref[rows, :] = a
        amid = a[HALF - 1:HALF, :]
        aend = a[CHUNK - 1:CHUNK, :]
        dmax = jnp.maximum(dmax, jnp.maximum(-amid, amid - aend))
    unsafe = jnp.max(dmax) > SAFE_DECAY

    y = ffn_block(y, FF_BLOCKS - 2)

    for c in range(nchunks):
        rows = pl.ds(c * CHUNK, CHUNK)
        a = a_ref[rows, :]
        k = z_ref[rows, Z_F:Z_V]
        aend = a[CHUNK - 1:CHUNK, :]
        qs = _silu(z_ref[rows, Z_Q:Z_F])
        qa_ref[rows, :] = (qs * jnp.exp(a)).astype(BF16)
        ks_ref[rows, :] = (k * jnp.exp(aend - a)).astype(BF16)
        dec_ref[c] = jnp.exp(aend)
        vb_ref[rows, :] = z_ref[rows, Z_V:Z_OG].astype(BF16)
        sog_ref[rows, :] = _silu(z_ref[rows, Z_OG:Z_CA])
        for hd in range(HG_HEADS):
            cols = slice(hd * HG_DK, (hd + 1) * HG_DK)
            p_ref[hd, rows, :] = _scores_factored(qs[:, cols], k[:, cols], a[:, cols], row, col).astype(BF16)

    y = ffn_block(y, FF_BLOCKS - 1)
    y_ref[...] = (_rms(y, fg_ref[...]) if final else y).reshape(spt, tps, D_MODEL)

    @pl.when(unsafe)
    def _():
        def chunk(c, carry):
            r0 = pl.multiple_of(c * CHUNK, CHUNK)
            rows = pl.ds(r0, CHUNK)
            for hd in range(HG_HEADS):
                lo = hd * HG_DK
                qh = _silu(z_ref[rows, Z_Q + lo:Z_Q + lo + HG_DK])
                ah = a_ref[rows, lo:lo + HG_DK]
                p_ref[hd, rows, :] = _scores_exact(z_ref, a_ref, r0, hd, qh, ah, row).astype(BF16)
            return carry

        lax.fori_loop(0, nchunks, chunk, 0)

    if carried:
        pl.when((tp == nt - 1) & (j >= 1))(functools.partial(store_state, 0))
        pl.when((tn == nt - 1) & (j < ns))(functools.partial(store_histories, 0))
        cfull_ref[pl.ds(CONV_PAD - CONV_HIST, CONV_HIST), :] = (
            cfull_ref[pl.ds(tps + CONV_PAD - CONV_HIST, CONV_HIST), :])
        pfull_ref[pl.ds(POOL_PAD - POOL_HIST, POOL_HIST), :] = (
            pfull_ref[pl.ds(tps + POOL_PAD - POOL_HIST, POOL_HIST), :])
    else:
        @pl.when(j < ns)
        def _():
            for seg in range(spt):
                store_histories(seg)


def _layer(li, pos0, x, mem_k, mem_v, hg0, conv0, pool0, p):
    b, t, _ = x.shape
    tps = min(t, TOKEN_TILE)
    spt = TOKEN_TILE // tps
    assert t % tps == 0 and tps % CHUNK == 0 and b % spt == 0 and (spt == 1 or t == tps)
    tt = spt * tps
    nt = t // tps
    ns = (b // spt) * nt
    nchunks = tt // CHUNK
    kern = functools.partial(_layer_kernel, li, pos0, spt, tps, nt, ns, li == DEPTH - 1)

    front = lambda j: jnp.minimum(j, ns - 1)
    back = lambda j: jnp.maximum(j - 1, 0)

    def resident(shape):
        zeros = (0,) * len(shape)
        return pl.BlockSpec((None,) + tuple(shape), lambda j: (li,) + zeros, pipeline_mode=pl.Buffered(1))

    stream_mode = dict(pipeline_mode=pl.Buffered(1)) if spt > 1 else {}

    def per_stream(shape, tile, **mode):
        zeros = (0,) * len(shape)
        return pl.BlockSpec((spt,) + tuple(shape), lambda j: (tile(j) // nt,) + zeros, **mode)

    def tile_spec(tile):
        return pl.BlockSpec((spt, tps, D_MODEL), lambda j: (tile(j) // nt, tile(j) % nt, 0))

    kv_spec = pl.BlockSpec((None, spt, N_MEM, D_MODEL), lambda j: (li, back(j) // nt, 0, 0), **stream_mode)
    seg_c, seg_p = CONV_PAD + tps, POOL_PAD + tps
    return pl.pallas_call(
        kern,
        grid=(ns + 1,),
        in_specs=[
            tile_spec(front), tile_spec(back),
            resident((1, D_MODEL)),
            resident((D_MODEL, D_IN)),
            pl.BlockSpec((DEPTH, HG_WIDTH), lambda j: (0, 0)),
            resident((1, HG_WIDTH)),
            resident((CONV_K, CONV_WIDTH)),
            resident((1, CONV_WIDTH)),
            resident((1, CONV_WIDTH)),
            resident((1, CONV_WIDTH)),
            resident((POOL_WIDTH, POOL_WIDTH)),
            resident((1, POOL_WIDTH)),
            resident((D_MODEL, D_MODEL)),
            per_stream((HG_HEADS, HG_DK, HG_DK), back, **stream_mode),
            per_stream((CONV_HIST, CONV_WIDTH), front),
            per_stream((POOL_HIST, POOL_WIDTH), front),
            resident((1, D_MODEL)),
            resident((D_MODEL, D_MODEL)),
            resident((D_MODEL, D_MODEL)),
            kv_spec, kv_spec,
            resident((1, D_MODEL)),
            resident((D_MODEL, D_FF)),
            resident((D_FF, D_MODEL)),
            pl.BlockSpec((1, D_MODEL), lambda j: (0, 0)),
        ],
        out_specs=[
            tile_spec(back),
            per_stream((HG_HEADS, HG_DK, HG_DK), back),
            per_stream((CONV_HIST, CONV_WIDTH), front),
            per_stream((POOL_HIST, POOL_WIDTH), front),
        ],
        out_shape=[
            jax.ShapeDtypeStruct(x.shape, F32),
            jax.ShapeDtypeStruct((b, HG_HEADS, HG_DK, HG_DK), F32),
            jax.ShapeDtypeStruct((b, CONV_HIST, CONV_WIDTH), F32),
            jax.ShapeDtypeStruct((b, POOL_HIST, POOL_WIDTH), F32),
        ],
        scratch_shapes=[
            pltpu.VMEM((tt, D_IN), F32),
            pltpu.VMEM((tt, HG_WIDTH), F32),
            pltpu.VMEM((tt, D_MODEL), BF16),
            pltpu.VMEM((tt, POOL_WIDTH), BF16),
            pltpu.VMEM((HG_HEADS, HG_DK, HG_DK), F32),
            pltpu.VMEM((spt * seg_c, CONV_WIDTH), F32),
            pltpu.VMEM((SUBLANES, spt * seg_c, CONV_WIDTH), F32),
            pltpu.VMEM((spt * seg_p, POOL_WIDTH), F32),
            pltpu.VMEM((spt * seg_p, POOL_WIDTH), F32),
            pltpu.VMEM((spt * seg_p, POOL_WIDTH), F32),
            pltpu.VMEM((spt * seg_p, 128), F32),
            pltpu.VMEM((HG_HEADS, tt, CHUNK), BF16),
            pltpu.VMEM((tt, HG_WIDTH), BF16),
            pltpu.VMEM((tt, HG_WIDTH), BF16),
            pltpu.VMEM((tt, HG_WIDTH), BF16),
            pltpu.VMEM((tt, HG_WIDTH), F32),
            pltpu.VMEM((nchunks, 1, HG_WIDTH), F32),
        ],
        compiler_params=pltpu.CompilerParams(
            dimension_semantics=("arbitrary",), vmem_limit_bytes=VMEM_LIMIT_V7X),
        name=f"layer{li}",
    )(x, x, p["norm_mix_g"], p["w_in"], p["lb_param"], p["hg_norm_g"], p["conv_w"], p["conv_b"],
      p["conv_ln_g"], p["conv_ln_b"], p["pool_w_bd"], p["pool_scale"], p["w_out"], hg0, conv0, pool0,
      p["norm_x_g"], p["xq_w"], p["xo_w"], mem_k, mem_v,
      p["norm_ffn_g"], p["w_up"], p["w_down"], p["final_g"])


def _memkv_kernel(m_ref, g_ref, wk_ref, wv_ref, k_ref, v_ref, kb_ref, vb_ref):
    h = _rms(m_ref[...], g_ref[...]).astype(BF16)
    k = _dot(h, wk_ref[...])
    v = _dot(h, wv_ref[...])
    k_ref[...] = k
    v_ref[...] = v
    kb_ref[...] = k.astype(BF16)
    vb_ref[...] = v.astype(BF16)


def _memkv(mem, p):
    b = mem.shape[0]
    n = b * N_MEM
    rt = 1024
    out_spec = pl.BlockSpec((None, rt, D_MODEL), lambda l, r: (l, r, 0))
    w_spec = pl.BlockSpec((None, D_MODEL, D_MODEL), lambda l, r: (l, 0, 0))
    return pl.pallas_call(
        _memkv_kernel,
        grid=(DEPTH, n // rt),
        in_specs=[
            pl.BlockSpec((rt, D_MODEL), lambda l, r: (r, 0)),
            pl.BlockSpec((None, 1, D_MODEL), lambda l, r: (l, 0, 0)),
            w_spec, w_spec,
        ],
        out_specs=[out_spec] * 4,
        out_shape=[jax.ShapeDtypeStruct((DEPTH, n, D_MODEL), F32)] * 2
        + [jax.ShapeDtypeStruct((DEPTH, n, D_MODEL), BF16)] * 2,
        compiler_params=pltpu.CompilerParams(
            dimension_semantics=("arbitrary", "arbitrary"), vmem_limit_bytes=VMEM_LIMIT_V7X),
        name="memkv",
    )(mem.reshape(n, D_MODEL), p["norm_mem_g"], p["xk_w"], p["xv_w"])


def _trunk(x, mem_k, mem_v, hg_state, conv_buf, pool_buf, pos0, p):
    new_hg, new_conv, new_pool = [], [], []
    for li in range(DEPTH):
        x, s_hg, s_conv, s_pool = _layer(li, pos0, x, mem_k, mem_v, hg_state[li], conv_buf[li], pool_buf[li], p)
        new_hg.append(s_hg)
        new_conv.append(s_conv)
        new_pool.append(s_pool)
    return x, jnp.stack(new_hg), jnp.stack(new_conv), jnp.stack(new_pool)


def kernel(x_prompt, x_sample, mem_prompt, state_hgrn, cache_conv, cache_pool, cache_mem_k, cache_mem_v,
           norm_mix_g, w_in, lb_param, hg_norm_g, conv_w, conv_b, conv_ln_g, conv_ln_b, pool_w, pool_scale,
           w_out, norm_x_g, norm_mem_g, xq_w, xk_w, xv_w, xo_w, norm_ffn_g, w_up, w_down, final_g):
    row = lambda a: a.reshape(a.shape[0], 1, a.shape[1])
    groups = pool_w.shape[1]
    pool_w_bd = (pool_w[:, :, :, None, :] * jnp.eye(groups, dtype=pool_w.dtype)[None, :, None, :, None]
                 ).reshape(DEPTH, POOL_WIDTH, POOL_WIDTH)
    p = dict(
        norm_mix_g=row(norm_mix_g), w_in=w_in.astype(BF16), lb_param=lb_param, hg_norm_g=row(hg_norm_g),
        conv_w=conv_w, conv_b=row(conv_b), conv_ln_g=row(conv_ln_g), conv_ln_b=row(conv_ln_b),
        pool_w_bd=pool_w_bd.astype(BF16), pool_scale=row(pool_scale), w_out=w_out.astype(BF16),
        norm_x_g=row(norm_x_g), norm_mem_g=row(norm_mem_g), xq_w=xq_w.astype(BF16), xk_w=xk_w.astype(BF16),
        xv_w=xv_w.astype(BF16), xo_w=xo_w.astype(BF16), norm_ffn_g=row(norm_ffn_g),
        w_up=w_up.astype(BF16), w_down=w_down.astype(BF16),
        final_g=final_g.reshape(1, D_MODEL))

    bp = x_prompt.shape[0]
    mem_k_p, mem_v_p, mem_kb, mem_vb = _memkv(mem_prompt, p)
    kv_shape = (DEPTH, bp, N_MEM, X_HEADS, X_HDIM)
    flat_kv = (DEPTH, bp, N_MEM, D_MODEL)

    dt = x_prompt.dtype
    hg0 = jnp.zeros((DEPTH, bp, HG_HEADS, HG_DK, HG_DK), dt)
    conv0 = jnp.zeros((DEPTH, bp, CONV_HIST, CONV_WIDTH), dt)
    pool0 = jnp.zeros((DEPTH, bp, POOL_HIST, POOL_WIDTH), dt)
    y_prompt, hg_p, conv_p, pool_p = _trunk(
        x_prompt, mem_kb.reshape(flat_kv), mem_vb.reshape(flat_kv), hg0, conv0, pool0, 0, p)

    bs = x_sample.shape[0]
    y_sample, hg_s, conv_s, pool_s = _trunk(
        x_sample, cache_mem_k.astype(BF16).reshape(DEPTH, bs, N_MEM, D_MODEL),
        cache_mem_v.astype(BF16).reshape(DEPTH, bs, N_MEM, D_MODEL),
        state_hgrn, cache_conv, cache_pool, PAST_LEN, p)

    return (y_prompt, y_sample, hg_p, conv_p, pool_p, mem_k_p.reshape(kv_shape), mem_v_p.reshape(kv_shape),
            hg_s, conv_s, pool_s)
```

```python
import functools

import jax
import jax.numpy as jnp
from jax import lax
from jax.experimental import pallas as pl
from jax.experimental.pallas import tpu as pltpu

F32 = jnp.float32
BF16 = jnp.bfloat16

D_MODEL = 1024
DEPTH = 4
CHUNK = 64
HALF = CHUNK // 2
N_MEM = 256
HG_DK = 128
HG_WIDTH = 512
HG_HEADS = 4
CONV_WIDTH = 256
CONV_K = 31
CONV_HIST = CONV_K - 1
POOL_WIDTH = 256
POOL_HIST = 15
D_IN = 4 * HG_WIDTH + 2 * CONV_WIDTH + POOL_WIDTH
X_HEADS = 4
X_HDIM = 256
D_FF = 4096
EPS = 1e-6
F_FLOOR = 1e-30
PAST_LEN = 2048

Z_Q, Z_F, Z_V, Z_OG, Z_CA, Z_CG, Z_PU = 0, 512, 1024, 1536, 2048, 2304, 2560

SUBLANES = 8
CONV_PAD = 32
POOL_PAD = 24
CONV_SHIFT_ROWS = 8

SAFE_DECAY = 80.0

TOKEN_TILE = 256
FF_BLOCKS = 4
VMEM_LIMIT_V7X = 58 * 1024 * 1024

NT_DIMS = (((1,), (1,)), ((), ()))
TN_DIMS = (((0,), (0,)), ((), ()))


def _rms(x, g):
    ms = jnp.mean(x * x, axis=-1, keepdims=True)
    return x * lax.rsqrt(ms + EPS) * g


def _silu(x):
    return x * jax.nn.sigmoid(x)


def _dot(a, b):
    return jnp.dot(a, b, preferred_element_type=F32)


def _scores_factored(qh, kh, ah, row, col):
    lower = row < HALF
    amid = ah[HALF - 1:HALF, :]
    ref = jnp.where(lower, 0.0, amid)
    ql = qh * jnp.exp(ah - ref)
    kl = kh * jnp.exp(ref - ah)
    emid = jnp.exp(amid)
    q2 = jnp.concatenate([jnp.where(lower, ql, 0.0), jnp.where(lower, 0.0, ql)], axis=1).astype(BF16)
    k2 = jnp.concatenate([kl, jnp.where(lower, kl * emid, kl)], axis=1).astype(BF16)
    p = lax.dot_general(q2, k2, NT_DIMS, preferred_element_type=F32)
    return jnp.where(row >= col, p, 0.0)


def _scores_exact(z_ref, a_ref, r0, hd, qh, ah, row):
    lane = lax.broadcasted_iota(jnp.int32, (1, HG_DK), 1)
    lo = hd * HG_DK

    def body(g, p):
        src = pl.ds(pl.multiple_of(r0 + g * SUBLANES, SUBLANES), SUBLANES)
        a_g = a_ref[src, lo:lo + HG_DK]
        k_g = z_ref[src, Z_F + lo:Z_F + lo + HG_DK]
        for i in range(SUBLANES):
            w = qh * k_g[i:i + 1, :] * jnp.exp(jnp.minimum(ah - a_g[i:i + 1, :], 0.0))
            p = jnp.where(lane == g * SUBLANES + i, jnp.sum(w, axis=-1, keepdims=True), p)
        return p

    p = lax.fori_loop(0, CHUNK // SUBLANES, body, jnp.zeros((CHUNK, HG_DK), F32))
    return jnp.where(row >= lane, p, 0.0)[:, :CHUNK]


def _layer_kernel(li, pos0, spt, tps, nt, ns, final,
                  xn_ref, xc_ref, gmix_ref, win_ref, lbp_ref, hgg_ref, cw_ref, cb_ref, clg_ref, clb_ref,
                  pw_ref, ps_ref, wout_ref, hg0_ref, conv0_ref, pool0_ref,
                  gx_ref, wq_ref, wo_ref, k_ref, v_ref,
                  gffn_ref, wup_ref, wdn_ref, fg_ref,
                  y_ref, hg_out_ref, conv_out_ref, pool_out_ref,
                  z_ref, a_ref, mix_ref, dp_ref, st_ref, cfull_ref, c8_ref, pfull_ref, s2_ref, s4_ref, s8_ref,
                  p_ref, qa_ref, ks_ref, vb_ref, sog_ref, dec_ref):
    j = pl.program_id(0)
    tn = lax.rem(jnp.minimum(j, ns - 1), nt)
    tp = lax.rem(jnp.maximum(j - 1, 0), nt)
    tt = spt * tps
    cps = tps // CHUNK
    nchunks = spt * cps
    seg_c = CONV_PAD + tps
    seg_p = POOL_PAD + tps
    carried = spt == 1

    @pl.when(j == 0)
    def _():
        for ref in (p_ref, qa_ref, ks_ref, vb_ref, sog_ref, dec_ref, mix_ref, dp_ref, pfull_ref, s2_ref, s4_ref):
            ref[...] = jnp.zeros(ref.shape, ref.dtype)

    def load_state(seg):
        for hd in range(HG_HEADS):
            st_ref[hd] = hg0_ref[seg, hd].T

    def store_state(seg):
        for hd in range(HG_HEADS):
            hg_out_ref[seg, hd] = st_ref[hd].T

    def load_histories(seg):
        cfull_ref[pl.ds(seg * seg_c + CONV_PAD - CONV_HIST, CONV_HIST), :] = conv0_ref[seg]
        pfull_ref[pl.ds(seg * seg_p + POOL_PAD - POOL_HIST, POOL_HIST), :] = pool0_ref[seg]

    def store_histories(seg):
        conv_out_ref[seg] = cfull_ref[pl.ds(seg * seg_c + tps + CONV_PAD - CONV_HIST, CONV_HIST), :]
        pool_out_ref[seg] = pfull_ref[pl.ds(seg * seg_p + tps + POOL_PAD - POOL_HIST, POOL_HIST), :]

    if carried:
        pl.when(tp == 0)(functools.partial(load_state, 0))
        pl.when(tn == 0)(functools.partial(load_histories, 0))
    else:
        for seg in range(spt):
            load_histories(seg)

    hgg = hgg_ref[...]
    for c in range(nchunks):
        rows = pl.ds(c * CHUNK, CHUNK)
        if not carried and c % cps == 0:
            load_state(c // cps)
        for hd in range(HG_HEADS):
            cols = slice(hd * HG_DK, (hd + 1) * HG_DK)
            st = st_ref[hd]
            vb = vb_ref[rows, cols]
            o = (_dot(p_ref[hd, rows, :], vb)
                 + lax.dot_general(qa_ref[rows, cols], st.astype(BF16), NT_DIMS, preferred_element_type=F32))
            st_ref[hd] = dec_ref[c, :, cols] * st + lax.dot_general(vb, ks_ref[rows, cols], TN_DIMS,
                                                                     preferred_element_type=F32)
            mix_ref[rows, cols] = (_rms(o, hgg[:, cols]) * sog_ref[rows, cols]).astype(BF16)
        if not carried and c % cps == cps - 1:
            store_state(c // cps)

    c_out = _dot(dp_ref[...], pw_ref[...]) * ps_ref[...]
    mix_ref[:, HG_WIDTH + CONV_WIDTH:] = c_out.astype(BF16)
    x1 = xc_ref[...].reshape(tt, D_MODEL) + _dot(mix_ref[...], wout_ref[...])

    h = _rms(xn_ref[...].reshape(tt, D_MODEL), gmix_ref[...]).astype(BF16)

    def project(*col_ranges):
        for lo, hi in col_ranges:
            z_ref[:, lo:hi] = _dot(h, win_ref[:, lo:hi])

    project((Z_F, Z_V), (Z_CA, Z_PU), (Z_PU, D_IN))

    hx = _rms(x1, gx_ref[...]).astype(BF16)
    q = (_dot(hx, wq_ref[...]) * (1.0 / 16.0)).astype(BF16)
    project((Z_Q, Z_F))
    heads = [(seg, hd) for seg in range(spt) for hd in range(X_HEADS)]
    scores = {}
    for seg, hd in heads:
        cols = slice(hd * X_HDIM, (hd + 1) * X_HDIM)
        scores[seg, hd] = lax.dot_general(q[seg * tps:(seg + 1) * tps, cols], k_ref[seg, :, cols], NT_DIMS,
                                          preferred_element_type=F32)
    project((Z_V, Z_OG), (Z_OG, Z_CA))
    seg_outs = []
    for seg in range(spt):
        outs = []
        for hd in range(X_HEADS):
            cols = slice(hd * X_HDIM, (hd + 1) * X_HDIM)
            sc = scores[seg, hd]
            pexp = jnp.exp(sc - jnp.max(sc, axis=-1, keepdims=True))
            prob = pexp / jnp.sum(pexp, axis=-1, keepdims=True)
            outs.append(_dot(prob.astype(BF16), v_ref[seg, :, cols]).astype(BF16))
        seg_outs.append(jnp.concatenate(outs, axis=1))
    attn = seg_outs[0] if spt == 1 else jnp.concatenate(seg_outs, axis=0)
    x2 = x1 + _dot(attn, wo_ref[...])

    hf = _rms(x2, gffn_ref[...]).astype(BF16)
    ff_blk = D_FF // FF_BLOCKS

    def ffn_block(y, c):
        blk = slice(c * ff_blk, (c + 1) * ff_blk)
        u = jnp.maximum(_dot(hf, wup_ref[:, blk]), 0.0)
        return y + _dot((u * u).astype(BF16), wdn_ref[blk, :])

    y = x2
    for c in range(FF_BLOCKS - 2):
        y = ffn_block(y, c)

    lbp = lbp_ref[...]
    e = jnp.exp(lbp - jnp.max(lbp, axis=0, keepdims=True))
    sm = e / jnp.sum(e, axis=0, keepdims=True)
    lb = jnp.zeros((1, HG_WIDTH), F32)
    for i in range(1, li + 1):
        lb = lb + sm[i:i + 1, :]
    oml = 1.0 - lb

    lane = lax.broadcasted_iota(jnp.int32, (1, 128), 1)
    low_lanes = lane < 64

    for seg in range(spt):
        seg_rows = pl.ds(seg * tps, tps)
        cfull_ref[pl.ds(seg * seg_c + CONV_PAD, tps), :] = (
            z_ref[seg_rows, Z_CA:Z_CG] * jax.nn.sigmoid(z_ref[seg_rows, Z_CG:Z_PU]))
        for b in range(SUBLANES):
            n = tps + (CONV_K - 1 - b) // SUBLANES * SUBLANES
            c8_ref[b, pl.ds(seg * seg_c + CONV_SHIFT_ROWS, n), :] = (
                cfull_ref[pl.ds(seg * seg_c + CONV_PAD - CONV_HIST + b, n), :])

    for c in range(nchunks):
        r0 = (c // cps) * seg_c + (c % cps) * CHUNK + CONV_SHIFT_ROWS
        dw = jnp.broadcast_to(cb_ref[...], (CHUNK, CONV_WIDTH))
        for tap in range(CONV_K):
            a, b = divmod(tap, SUBLANES)
            dw = dw + c8_ref[b, pl.ds(r0 + SUBLANES * a, CHUNK), :] * cw_ref[tap:tap + 1, :]
        mu = jnp.mean(dw, axis=-1, keepdims=True)
        xc = dw - mu
        var = jnp.mean(xc * xc, axis=-1, keepdims=True)
        yb = xc * lax.rsqrt(var + EPS) * clg_ref[...] + clb_ref[...]
        mix_ref[pl.ds(c * CHUNK, CHUNK), HG_WIDTH:HG_WIDTH + CONV_WIDTH] = _silu(yb).astype(BF16)

    for seg in range(spt):
        pfull_ref[pl.ds(seg * seg_p + POOL_PAD, tps), :] = z_ref[pl.ds(seg * tps, tps), Z_PU:D_IN]
    n = spt * seg_p - SUBLANES
    s2_ref[pl.ds(8, n), :] = pfull_ref[pl.ds(8, n), :] + pfull_ref[pl.ds(7, n), :]
    s4_ref[pl.ds(8, n), :] = s2_ref[pl.ds(8, n), :] + s2_ref[pl.ds(6, n), :]
    s8_ref[pl.ds(8, n), :] = s4_ref[pl.ds(8, n), 128:256] + s4_ref[pl.ds(4, n), 128:256]
    posf = (pos0 + 1 + tn * tps + lax.broadcasted_iota(jnp.int32, (tps, 1), 0)).astype(F32)
    cnt_a = jnp.minimum(posf, jnp.where(low_lanes, 2.0, 4.0))
    cnt_b = jnp.minimum(posf, jnp.where(low_lanes, 8.0, 16.0))
    for seg in range(spt):
        cur = pl.ds(seg * seg_p + POOL_PAD, tps)
        s16 = s8_ref[cur, :] + s8_ref[pl.ds(seg * seg_p + POOL_PAD - 8, tps), :]
        win_a = jnp.where(low_lanes, s2_ref[cur, 0:128], s4_ref[cur, 0:128])
        win_b = jnp.where(low_lanes, s8_ref[cur, :], s16)
        seg_rows = pl.ds(seg * tps, tps)
        dp_ref[seg_rows, 0:128] = (win_a / cnt_a - pfull_ref[cur, 0:128]).astype(BF16)
        dp_ref[seg_rows, 128:256] = (win_b / cnt_b - pfull_ref[cur, 128:256]).astype(BF16)

    row = lax.broadcasted_iota(jnp.int32, (CHUNK, 1), 0)
    col = lax.broadcasted_iota(jnp.int32, (1, CHUNK), 1)
    tri = (row >= col).astype(BF16)
    dmax = jnp.zeros((1, HG_WIDTH), F32)
    for c in range(nchunks):
        rows = pl.ds(c * CHUNK, CHUNK)
        fl = z_ref[rows, Z_F:Z_V]
        f_gate = lb + oml * jax.nn.sigmoid(fl)
        log_f = jnp.log(jnp.maximum(f_gate, F_FLOOR))
        hi = log_f.astype(BF16)
        lo = (log_f - hi.astype(F32)).astype(BF16)
        cs = _dot(tri, jnp.concatenate([hi, lo], axis=1))
        a = cs[:, :HG_WIDTH] + cs[:, HG_WIDTH:]
        k = oml * jax.nn.sigmoid(-fl)
        z_ref[rows, Z_F:Z_V] = k
        a_ref[rows, :] = a
        amid = a[HALF - 1:HALF, :]
        aend = a[CHUNK - 1:CHUNK, :]
        dmax = jnp.maximum(dmax, jnp.maximum(-amid, amid - aend))
    unsafe = jnp.max(dmax) > SAFE_DECAY

    y = ffn_block(y, FF_BLOCKS - 2)

    for c in range(nchunks):
        rows = pl.ds(c * CHUNK, CHUNK)
        a = a_ref[rows, :]
        k = z_ref[rows, Z_F:Z_V]
        aend = a[CHUNK - 1:CHUNK, :]
        qs = _silu(z_ref[rows, Z_Q:Z_F])
        qa_ref[rows, :] = (qs * jnp.exp(a)).astype(BF16)
        ks_ref[rows, :] = (k * jnp.exp(aend - a)).astype(BF16)
        dec_ref[c] = jnp.exp(aend)
        vb_ref[rows, :] = z_ref[rows, Z_V:Z_OG].astype(BF16)
        sog_ref[rows, :] = _silu(z_ref[rows, Z_OG:Z_CA])
        for hd in range(HG_HEADS):
            cols = slice(hd * HG_DK, (hd + 1) * HG_DK)
            p_ref[hd, rows, :] = _scores_factored(qs[:, cols], k[:, cols], a[:, cols], row, col).astype(BF16)

    y = ffn_block(y, FF_BLOCKS - 1)
    y_ref[...] = (_rms(y, fg_ref[...]) if final else y).reshape(spt, tps, D_MODEL)

    @pl.when(unsafe)
    def _():
        def chunk(c, carry):
            r0 = pl.multiple_of(c * CHUNK, CHUNK)
            rows = pl.ds(r0, CHUNK)
            for hd in range(HG_HEADS):
                lo = hd * HG_DK
                qh = _silu(z_ref[rows, Z_Q + lo:Z_Q + lo + HG_DK])
                ah = a_ref[rows, lo:lo + HG_DK]
                p_ref[hd, rows, :] = _scores_exact(z_ref, a_ref, r0, hd, qh, ah, row).astype(BF16)
            return carry

        lax.fori_loop(0, nchunks, chunk, 0)

    if carried:
        pl.when((tp == nt - 1) & (j >= 1))(functools.partial(store_state, 0))
        pl.when((tn == nt - 1) & (j < ns))(functools.partial(store_histories, 0))
        cfull_ref[pl.ds(CONV_PAD - CONV_HIST, CONV_HIST), :] = (
            cfull_ref[pl.ds(tps + CONV_PAD - CONV_HIST, CONV_HIST), :])
        pfull_ref[pl.ds(POOL_PAD - POOL_HIST, POOL_HIST), :] = (
            pfull_ref[pl.ds(tps + POOL_PAD - POOL_HIST, POOL_HIST), :])
    else:
        @pl.when(j < ns)
        def _():
            for seg in range(spt):
                store_histories(seg)


def _layer(li, pos0, x, mem_k, mem_v, hg0, conv0, pool0, p):
    b, t, _ = x.shape
    tps = min(t, TOKEN_TILE)
    spt = TOKEN_TILE // tps
    assert t % tps == 0 and tps % CHUNK == 0 and b % spt == 0 and (spt == 1 or t == tps)
    tt = spt * tps
    nt = t // tps
    ns = (b // spt) * nt
    nchunks = tt // CHUNK
    kern = functools.partial(_layer_kernel, li, pos0, spt, tps, nt, ns, li == DEPTH - 1)

    front = lambda j: jnp.minimum(j, ns - 1)
    back = lambda j: jnp.maximum(j - 1, 0)

    def resident(shape):
        zeros = (0,) * len(shape)
        return pl.BlockSpec((None,) + tuple(shape), lambda j: (li,) + zeros, pipeline_mode=pl.Buffered(1))

    stream_mode = dict(pipeline_mode=pl.Buffered(1)) if spt > 1 else {}

    def per_stream(shape, tile, **mode):
        zeros = (0,) * len(shape)
        return pl.BlockSpec((spt,) + tuple(shape), lambda j: (tile(j) // nt,) + zeros, **mode)

    def tile_spec(tile):
        return pl.BlockSpec((spt, tps, D_MODEL), lambda j: (tile(j) // nt, tile(j) % nt, 0))

    kv_spec = pl.BlockSpec((None, spt, N_MEM, D_MODEL), lambda j: (li, back(j) // nt, 0, 0), **stream_mode)
    seg_c, seg_p = CONV_PAD + tps, POOL_PAD + tps
    return pl.pallas_call(
        kern,
        grid=(ns + 1,),
        in_specs=[
            tile_spec(front), tile_spec(back),
            resident((1, D_MODEL)),
            resident((D_MODEL, D_IN)),
            pl.BlockSpec((DEPTH, HG_WIDTH), lambda j: (0, 0)),
            resident((1, HG_WIDTH)),
            resident((CONV_K, CONV_WIDTH)),
            resident((1, CONV_WIDTH)),
            resident((1, CONV_WIDTH)),
            resident((1, CONV_WIDTH)),
            resident((POOL_WIDTH, POOL_WIDTH)),
            resident((1, POOL_WIDTH)),
            resident((D_MODEL, D_MODEL)),
            per_stream((HG_HEADS, HG_DK, HG_DK), back, **stream_mode),
            per_stream((CONV_HIST, CONV_WIDTH), front),
            per_stream((POOL_HIST, POOL_WIDTH), front),
            resident((1, D_MODEL)),
            resident((D_MODEL, D_MODEL)),
            resident((D_MODEL, D_MODEL)),
            kv_spec, kv_spec,
            resident((1, D_MODEL)),
            resident((D_MODEL, D_FF)),
            resident((D_FF, D_MODEL)),
            pl.BlockSpec((1, D_MODEL), lambda j: (0, 0)),
        ],
        out_specs=[
            tile_spec(back),
            per_stream((HG_HEADS, HG_DK, HG_DK), back),
            per_stream((CONV_HIST, CONV_WIDTH), front),
            per_stream((POOL_HIST, POOL_WIDTH), front),
        ],
        out_shape=[
            jax.ShapeDtypeStruct(x.shape, F32),
            jax.ShapeDtypeStruct((b, HG_HEADS, HG_DK, HG_DK), F32),
            jax.ShapeDtypeStruct((b, CONV_HIST, CONV_WIDTH), F32),
            jax.ShapeDtypeStruct((b, POOL_HIST, POOL_WIDTH), F32),
        ],
        scratch_shapes=[
            pltpu.VMEM((tt, D_IN), F32),
            pltpu.VMEM((tt, HG_WIDTH), F32),
            pltpu.VMEM((tt, D_MODEL), BF16),
            pltpu.VMEM((tt, POOL_WIDTH), BF16),
            pltpu.VMEM((HG_HEADS, HG_DK, HG_DK), F32),
            pltpu.VMEM((spt * seg_c, CONV_WIDTH), F32),
            pltpu.VMEM((SUBLANES, spt * seg_c, CONV_WIDTH), F32),
            pltpu.VMEM((spt * seg_p, POOL_WIDTH), F32),
            pltpu.VMEM((spt * seg_p, POOL_WIDTH), F32),
            pltpu.VMEM((spt * seg_p, POOL_WIDTH), F32),
            pltpu.VMEM((spt * seg_p, 128), F32),
            pltpu.VMEM((HG_HEADS, tt, CHUNK), BF16),
            pltpu.VMEM((tt, HG_WIDTH), BF16),
            pltpu.VMEM((tt, HG_WIDTH), BF16),
            pltpu.VMEM((tt, HG_WIDTH), BF16),
            pltpu.VMEM((tt, HG_WIDTH), F32),
            pltpu.VMEM((nchunks, 1, HG_WIDTH), F32),
        ],
        compiler_params=pltpu.CompilerParams(
            dimension_semantics=("arbitrary",), vmem_limit_bytes=VMEM_LIMIT_V7X),
        name=f"layer{li}",
    )(x, x, p["norm_mix_g"], p["w_in"], p["lb_param"], p["hg_norm_g"], p["conv_w"], p["conv_b"],
      p["conv_ln_g"], p["conv_ln_b"], p["pool_w_bd"], p["pool_scale"], p["w_out"], hg0, conv0, pool0,
      p["norm_x_g"], p["xq_w"], p["xo_w"], mem_k, mem_v,
      p["norm_ffn_g"], p["w_up"], p["w_down"], p["final_g"])


def _memkv_kernel(m_ref, g_ref, wk_ref, wv_ref, k_ref, v_ref, kb_ref, vb_ref):
    h = _rms(m_ref[...], g_ref[...]).astype(BF16)
    k = _dot(h, wk_ref[...])
    v = _dot(h, wv_ref[...])
    k_ref[...] = k
    v_ref[...] = v
    kb_ref[...] = k.astype(BF16)
    vb_ref[...] = v.astype(BF16)


def _memkv(mem, p):
    b = mem.shape[0]
    n = b * N_MEM
    rt = 1024
    out_spec = pl.BlockSpec((None, rt, D_MODEL), lambda l, r: (l, r, 0))
    w_spec = pl.BlockSpec((None, D_MODEL, D_MODEL), lambda l, r: (l, 0, 0))
    return pl.pallas_call(
        _memkv_kernel,
        grid=(DEPTH, n // rt),
        in_specs=[
            pl.BlockSpec((rt, D_MODEL), lambda l, r: (r, 0)),
            pl.BlockSpec((None, 1, D_MODEL), lambda l, r: (l, 0, 0)),
            w_spec, w_spec,
        ],
        out_specs=[out_spec] * 4,
        out_shape=[jax.ShapeDtypeStruct((DEPTH, n, D_MODEL), F32)] * 2
        + [jax.ShapeDtypeStruct((DEPTH, n, D_MODEL), BF16)] * 2,
        compiler_params=pltpu.CompilerParams(
            dimension_semantics=("arbitrary", "arbitrary"), vmem_limit_bytes=VMEM_LIMIT_V7X),
        name="memkv",
    )(mem.reshape(n, D_MODEL), p["norm_mem_g"], p["xk_w"], p["xv_w"])


def _trunk(x, mem_k, mem_v, hg_state, conv_buf, pool_buf, pos0, p):
    new_hg, new_conv, new_pool = [], [], []
    for li in range(DEPTH):
        x, s_hg, s_conv, s_pool = _layer(li, pos0, x, mem_k, mem_v, hg_state[li], conv_buf[li], pool_buf[li], p)
        new_hg.append(s_hg)
        new_conv.append(s_conv)
        new_pool.append(s_pool)
    return x, jnp.stack(new_hg), jnp.stack(new_conv), jnp.stack(new_pool)


def kernel(x_prompt, x_sample, mem_prompt, state_hgrn, cache_conv, cache_pool, cache_mem_k, cache_mem_v,
           norm_mix_g, w_in, lb_param, hg_norm_g, conv_w, conv_b, conv_ln_g, conv_ln_b, pool_w, pool_scale,
           w_out, norm_x_g, norm_mem_g, xq_w, xk_w, xv_w, xo_w, norm_ffn_g, w_up, w_down, final_g):
    row = lambda a: a.reshape(a.shape[0], 1, a.shape[1])
    groups = pool_w.shape[1]
    pool_w_bd = (pool_w[:, :, :, None, :] * jnp.eye(groups, dtype=pool_w.dtype)[None, :, None, :, None]
                 ).reshape(DEPTH, POOL_WIDTH, POOL_WIDTH)
    p = dict(
        norm_mix_g=row(norm_mix_g), w_in=w_in.astype(BF16), lb_param=lb_param, hg_norm_g=row(hg_norm_g),
        conv_w=conv_w, conv_b=row(conv_b), conv_ln_g=row(conv_ln_g), conv_ln_b=row(conv_ln_b),
        pool_w_bd=pool_w_bd.astype(BF16), pool_scale=row(pool_scale), w_out=w_out.astype(BF16),
        norm_x_g=row(norm_x_g), norm_mem_g=row(norm_mem_g), xq_w=xq_w.astype(BF16), xk_w=xk_w.astype(BF16),
        xv_w=xv_w.astype(BF16), xo_w=xo_w.astype(BF16), norm_ffn_g=row(norm_ffn_g),
        w_up=w_up.astype(BF16), w_down=w_down.astype(BF16),
        final_g=final_g.reshape(1, D_MODEL))

    bp = x_prompt.shape[0]
    mem_k_p, mem_v_p, mem_kb, mem_vb = _memkv(mem_prompt, p)
    kv_shape = (DEPTH, bp, N_MEM, X_HEADS, X_HDIM)
    flat_kv = (DEPTH, bp, N_MEM, D_MODEL)

    dt = x_prompt.dtype
    hg0 = jnp.zeros((DEPTH, bp, HG_HEADS, HG_DK, HG_DK), dt)
    conv0 = jnp.zeros((DEPTH, bp, CONV_HIST, CONV_WIDTH), dt)
    pool0 = jnp.zeros((DEPTH, bp, POOL_HIST, POOL_WIDTH), dt)
    y_prompt, hg_p, conv_p, pool_p = _trunk(
        x_prompt, mem_kb.reshape(flat_kv), mem_vb.reshape(flat_kv), hg0, conv0, pool0, 0, p)

    bs = x_sample.shape[0]
    y_sample, hg_s, conv_s, pool_s = _trunk(
        x_sample, cache_mem_k.astype(BF16).reshape(DEPTH, bs, N_MEM, D_MODEL),
        cache_mem_v.astype(BF16).reshape(DEPTH, bs, N_MEM, D_MODEL),
        state_hgrn, cache_conv, cache_pool, PAST_LEN, p)

    return (y_prompt, y_sample, hg_p, conv_p, pool_p, mem_k_p.reshape(kv_shape), mem_v_p.reshape(kv_shape),
            hg_s, conv_s, pool_s)
```

```python
import functools

import jax
import jax.numpy as jnp
from jax import lax
from jax.experimental import pallas as pl
from jax.experimental.pallas import tpu as pltpu

F32 = jnp.float32
BF16 = jnp.bfloat16

D_MODEL = 1024
DEPTH = 4
CHUNK = 64
HALF = CHUNK // 2
N_MEM = 256
HG_DK = 128
HG_WIDTH = 512
HG_HEADS = 4
CONV_WIDTH = 256
CONV_K = 31
CONV_HIST = CONV_K - 1
POOL_WIDTH = 256
POOL_HIST = 15
D_IN = 4 * HG_WIDTH + 2 * CONV_WIDTH + POOL_WIDTH
X_HEADS = 4
X_HDIM = 256
D_FF = 4096
EPS = 1e-6
F_FLOOR = 1e-30
PAST_LEN = 2048

Z_Q, Z_F, Z_V, Z_OG, Z_CA, Z_CG, Z_PU = 0, 512, 1024, 1536, 2048, 2304, 2560

SUBLANES = 8
CONV_PAD = 32
POOL_PAD = 24
CONV_SHIFT_ROWS = 8

SAFE_DECAY = 80.0

TOKEN_TILE = 256
FF_BLOCKS = 4
VMEM_LIMIT_V7X = 58 * 1024 * 1024

NT_DIMS = (((1,), (1,)), ((), ()))
TN_DIMS = (((0,), (0,)), ((), ()))


def _rms(x, g):
    ms = jnp.mean(x * x, axis=-1, keepdims=True)
    return x * lax.rsqrt(ms + EPS) * g


def _silu(x):
    return x * jax.nn.sigmoid(x)


def _dot(a, b):
    return jnp.dot(a, b, preferred_element_type=F32)


def _scores_factored(qh, kh, ah, row, col):
    lower = row < HALF
    amid = ah[HALF - 1:HALF, :]
    ref = jnp.where(lower, 0.0, amid)
    ql = qh * jnp.exp(ah - ref)
    kl = kh * jnp.exp(ref - ah)
    emid = jnp.exp(amid)
    q2 = jnp.concatenate([jnp.where(lower, ql, 0.0), jnp.where(lower, 0.0, ql)], axis=1).astype(BF16)
    k2 = jnp.concatenate([kl, jnp.where(lower, kl * emid, kl)], axis=1).astype(BF16)
    p = lax.dot_general(q2, k2, NT_DIMS, preferred_element_type=F32)
    return jnp.where(row >= col, p, 0.0)


def _scores_exact(z_ref, a_ref, r0, hd, qh, ah, row):
    lane = lax.broadcasted_iota(jnp.int32, (1, HG_DK), 1)
    lo = hd * HG_DK

    def body(g, p):
        src = pl.ds(pl.multiple_of(r0 + g * SUBLANES, SUBLANES), SUBLANES)
        a_g = a_ref[src, lo:lo + HG_DK]
        k_g = z_ref[src, Z_F + lo:Z_F + lo + HG_DK]
        for i in range(SUBLANES):
            w = qh * k_g[i:i + 1, :] * jnp.exp(jnp.minimum(ah - a_g[i:i + 1, :], 0.0))
            p = jnp.where(lane == g * SUBLANES + i, jnp.sum(w, axis=-1, keepdims=True), p)
        return p

    p = lax.fori_loop(0, CHUNK // SUBLANES, body, jnp.zeros((CHUNK, HG_DK), F32))
    return jnp.where(row >= lane, p, 0.0)[:, :CHUNK]


def _layer_kernel(li, pos0, spt, tps, nt, ns, final,
                  xn_ref, gmix_ref, win_ref, lbp_ref, hgg_ref, cw_ref, cb_ref, clg_ref, clb_ref,
                  pw_ref, ps_ref, wout_ref, hg0_ref, conv0_ref, pool0_ref,
                  gx_ref, wq_ref, wo_ref, k_ref, v_ref,
                  gffn_ref, wup_ref, wdn_ref, fg_ref,
                  y_ref, hg_out_ref, conv_out_ref, pool_out_ref,
                  z_ref, a_ref, mix_ref, dp_ref, st_ref, cfull_ref, c8_ref, pfull_ref, s2_ref, s4_ref, s8_ref,
                  p_ref, qa_ref, ks_ref, vb_ref, sog_ref, dec_ref, xkeep_ref):
    j = pl.program_id(0)
    tn = lax.rem(jnp.minimum(j, ns - 1), nt)
    tp = lax.rem(jnp.maximum(j - 1, 0), nt)
    tt = spt * tps
    cps = tps // CHUNK
    nchunks = spt * cps
    seg_c = CONV_PAD + tps
    seg_p = POOL_PAD + tps
    carried = spt == 1

    @pl.when(j == 0)
    def _():
        for ref in (p_ref, qa_ref, ks_ref, vb_ref, sog_ref, dec_ref, mix_ref, dp_ref, pfull_ref, s2_ref, s4_ref,
                    xkeep_ref):
            ref[...] = jnp.zeros(ref.shape, ref.dtype)

    def load_state(seg):
        for hd in range(HG_HEADS):
            st_ref[hd] = hg0_ref[seg, hd].T

    def store_state(seg):
        for hd in range(HG_HEADS):
            hg_out_ref[seg, hd] = st_ref[hd].T

    def load_histories(seg):
        cfull_ref[pl.ds(seg * seg_c + CONV_PAD - CONV_HIST, CONV_HIST), :] = conv0_ref[seg]
        pfull_ref[pl.ds(seg * seg_p + POOL_PAD - POOL_HIST, POOL_HIST), :] = pool0_ref[seg]

    def store_histories(seg):
        conv_out_ref[seg] = cfull_ref[pl.ds(seg * seg_c + tps + CONV_PAD - CONV_HIST, CONV_HIST), :]
        pool_out_ref[seg] = pfull_ref[pl.ds(seg * seg_p + tps + POOL_PAD - POOL_HIST, POOL_HIST), :]

    if carried:
        pl.when(tp == 0)(functools.partial(load_state, 0))
        pl.when(tn == 0)(functools.partial(load_histories, 0))
    else:
        for seg in range(spt):
            load_histories(seg)

    hgg = hgg_ref[...]
    for c in range(nchunks):
        rows = pl.ds(c * CHUNK, CHUNK)
        if not carried and c % cps == 0:
            load_state(c // cps)
        for hd in range(HG_HEADS):
            cols = slice(hd * HG_DK, (hd + 1) * HG_DK)
            st = st_ref[hd]
            vb = vb_ref[rows, cols]
            o = (_dot(p_ref[hd, rows, :], vb)
                 + lax.dot_general(qa_ref[rows, cols], st.astype(BF16), NT_DIMS, preferred_element_type=F32))
            st_ref[hd] = dec_ref[c, :, cols] * st + lax.dot_general(vb, ks_ref[rows, cols], TN_DIMS,
                                                                     preferred_element_type=F32)
            mix_ref[rows, cols] = (_rms(o, hgg[:, cols]) * sog_ref[rows, cols]).astype(BF16)
        if not carried and c % cps == cps - 1:
            store_state(c // cps)

    c_out = _dot(dp_ref[...], pw_ref[...]) * ps_ref[...]
    mix_ref[:, HG_WIDTH + CONV_WIDTH:] = c_out.astype(BF16)
    x1 = xkeep_ref[...] + _dot(mix_ref[...], wout_ref[...])

    h = _rms(xn_ref[...].reshape(tt, D_MODEL), gmix_ref[...]).astype(BF16)

    def project(*col_ranges):
        for lo, hi in col_ranges:
            z_ref[:, lo:hi] = _dot(h, win_ref[:, lo:hi])

    project((Z_F, Z_V), (Z_CA, Z_PU), (Z_PU, D_IN))

    hx = _rms(x1, gx_ref[...]).astype(BF16)
    q = (_dot(hx, wq_ref[...]) * (1.0 / 16.0)).astype(BF16)
    project((Z_Q, Z_F))
    heads = [(seg, hd) for seg in range(spt) for hd in range(X_HEADS)]
    scores = {}
    for seg, hd in heads:
        cols = slice(hd * X_HDIM, (hd + 1) * X_HDIM)
        scores[seg, hd] = lax.dot_general(q[seg * tps:(seg + 1) * tps, cols], k_ref[seg, :, cols], NT_DIMS,
                                          preferred_element_type=F32)
    project((Z_V, Z_OG), (Z_OG, Z_CA))
    seg_outs = []
    for seg in range(spt):
        outs = []
        for hd in range(X_HEADS):
            cols = slice(hd * X_HDIM, (hd + 1) * X_HDIM)
            sc = scores[seg, hd]
            pexp = jnp.exp(sc - jnp.max(sc, axis=-1, keepdims=True))
            prob = pexp / jnp.sum(pexp, axis=-1, keepdims=True)
            outs.append(_dot(prob.astype(BF16), v_ref[seg, :, cols]).astype(BF16))
        seg_outs.append(jnp.concatenate(outs, axis=1))
    attn = seg_outs[0] if spt == 1 else jnp.concatenate(seg_outs, axis=0)
    x2 = x1 + _dot(attn, wo_ref[...])

    hf = _rms(x2, gffn_ref[...]).astype(BF16)
    ff_blk = D_FF // FF_BLOCKS

    def ffn_block(y, c):
        blk = slice(c * ff_blk, (c + 1) * ff_blk)
        u = jnp.maximum(_dot(hf, wup_ref[:, blk]), 0.0)
        return y + _dot((u * u).astype(BF16), wdn_ref[blk, :])

    y = x2
    for c in range(FF_BLOCKS - 2):
        y = ffn_block(y, c)

    lbp = lbp_ref[...]
    e = jnp.exp(lbp - jnp.max(lbp, axis=0, keepdims=True))
    sm = e / jnp.sum(e, axis=0, keepdims=True)
    lb = jnp.zeros((1, HG_WIDTH), F32)
    for i in range(1, li + 1):
        lb = lb + sm[i:i + 1, :]
    oml = 1.0 - lb

    lane = lax.broadcasted_iota(jnp.int32, (1, 128), 1)
    low_lanes = lane < 64

    for seg in range(spt):
        seg_rows = pl.ds(seg * tps, tps)
        cfull_ref[pl.ds(seg * seg_c + CONV_PAD, tps), :] = (
            z_ref[seg_rows, Z_CA:Z_CG] * jax.nn.sigmoid(z_ref[seg_rows, Z_CG:Z_PU]))
        for b in range(SUBLANES):
            n = tps + (CONV_K - 1 - b) // SUBLANES * SUBLANES
            c8_ref[b, pl.ds(seg * seg_c + CONV_SHIFT_ROWS, n), :] = (
                cfull_ref[pl.ds(seg * seg_c + CONV_PAD - CONV_HIST + b, n), :])

    for c in range(nchunks):
        r0 = (c // cps) * seg_c + (c % cps) * CHUNK + CONV_SHIFT_ROWS
        dw = jnp.broadcast_to(cb_ref[...], (CHUNK, CONV_WIDTH))
        for tap in range(CONV_K):
            a, b = divmod(tap, SUBLANES)
            dw = dw + c8_ref[b, pl.ds(r0 + SUBLANES * a, CHUNK), :] * cw_ref[tap:tap + 1, :]
        mu = jnp.mean(dw, axis=-1, keepdims=True)
        xc = dw - mu
        var = jnp.mean(xc * xc, axis=-1, keepdims=True)
        yb = xc * lax.rsqrt(var + EPS) * clg_ref[...] + clb_ref[...]
        mix_ref[pl.ds(c * CHUNK, CHUNK), HG_WIDTH:HG_WIDTH + CONV_WIDTH] = _silu(yb).astype(BF16)

    for seg in range(spt):
        pfull_ref[pl.ds(seg * seg_p + POOL_PAD, tps), :] = z_ref[pl.ds(seg * tps, tps), Z_PU:D_IN]
    n = spt * seg_p - SUBLANES
    s2_ref[pl.ds(8, n), :] = pfull_ref[pl.ds(8, n), :] + pfull_ref[pl.ds(7, n), :]
    s4_ref[pl.ds(8, n), :] = s2_ref[pl.ds(8, n), :] + s2_ref[pl.ds(6, n), :]
    s8_ref[pl.ds(8, n), :] = s4_ref[pl.ds(8, n), 128:256] + s4_ref[pl.ds(4, n), 128:256]
    posf = (pos0 + 1 + tn * tps + lax.broadcasted_iota(jnp.int32, (tps, 1), 0)).astype(F32)
    cnt_a = jnp.minimum(posf, jnp.where(low_lanes, 2.0, 4.0))
    cnt_b = jnp.minimum(posf, jnp.where(low_lanes, 8.0, 16.0))
    for seg in range(spt):
        cur = pl.ds(seg * seg_p + POOL_PAD, tps)
        s16 = s8_ref[cur, :] + s8_ref[pl.ds(seg * seg_p + POOL_PAD - 8, tps), :]
        win_a = jnp.where(low_lanes, s2_ref[cur, 0:128], s4_ref[cur, 0:128])
        win_b = jnp.where(low_lanes, s8_ref[cur, :], s16)
        seg_rows = pl.ds(seg * tps, tps)
        dp_ref[seg_rows, 0:128] = (win_a / cnt_a - pfull_ref[cur, 0:128]).astype(BF16)
        dp_ref[seg_rows, 128:256] = (win_b / cnt_b - pfull_ref[cur, 128:256]).astype(BF16)

    row = lax.broadcasted_iota(jnp.int32, (CHUNK, 1), 0)
    col = lax.broadcasted_iota(jnp.int32, (1, CHUNK), 1)
    tri = (row >= col).astype(BF16)
    dmax = jnp.zeros((1, HG_WIDTH), F32)
    for c in range(nchunks):
        rows = pl.ds(c * CHUNK, CHUNK)
        fl = z_ref[rows, Z_F:Z_V]
        f_gate = lb + oml * jax.nn.sigmoid(fl)
        log_f = jnp.log(jnp.maximum(f_gate, F_FLOOR))
        hi = log_f.astype(BF16)
        lo = (log_f - hi.astype(F32)).astype(BF16)
        cs = _dot(tri, jnp.concatenate([hi, lo], axis=1))
        a = cs[:, :HG_WIDTH] + cs[:, HG_WIDTH:]
        k = oml * jax.nn.sigmoid(-fl)
        z_ref[rows, Z_F:Z_V] = k
        a_ref[rows, :] = a
        amid = a[HALF - 1:HALF, :]
        aend = a[CHUNK - 1:CHUNK, :]
        dmax = jnp.maximum(dmax, jnp.maximum(-amid, amid - aend))
    unsafe = jnp.max(dmax) > SAFE_DECAY

    y = ffn_block(y, FF_BLOCKS - 2)

    for c in range(nchunks):
        rows = pl.ds(c * CHUNK, CHUNK)
        a = a_ref[rows, :]
        k = z_ref[rows, Z_F:Z_V]
        aend = a[CHUNK - 1:CHUNK, :]
        qs = _silu(z_ref[rows, Z_Q:Z_F])
        qa_ref[rows, :] = (qs * jnp.exp(a)).astype(BF16)
        ks_ref[rows, :] = (k * jnp.exp(aend - a)).astype(BF16)
        dec_ref[c] = jnp.exp(aend)
        vb_ref[rows, :] = z_ref[rows, Z_V:Z_OG].astype(BF16)
        sog_ref[rows, :] = _silu(z_ref[rows, Z_OG:Z_CA])
        for hd in range(HG_HEADS):
            cols = slice(hd * HG_DK, (hd + 1) * HG_DK)
            p_ref[hd, rows, :] = _scores_factored(qs[:, cols], k[:, cols], a[:, cols], row, col).astype(BF16)

    y = ffn_block(y, FF_BLOCKS - 1)
    y_ref[...] = (_rms(y, fg_ref[...]) if final else y).reshape(spt, tps, D_MODEL)
    xkeep_ref[...] = xn_ref[...].reshape(tt, D_MODEL)

    @pl.when(unsafe)
    def _():
        def chunk(c, carry):
            r0 = pl.multiple_of(c * CHUNK, CHUNK)
            rows = pl.ds(r0, CHUNK)
            for hd in range(HG_HEADS):
                lo = hd * HG_DK
                qh = _silu(z_ref[rows, Z_Q + lo:Z_Q + lo + HG_DK])
                ah = a_ref[rows, lo:lo + HG_DK]
                p_ref[hd, rows, :] = _scores_exact(z_ref, a_ref, r0, hd, qh, ah, row).astype(BF16)
            return carry

        lax.fori_loop(0, nchunks, chunk, 0)

    if carried:
        pl.when((tp == nt - 1) & (j >= 1))(functools.partial(store_state, 0))
        pl.when((tn == nt - 1) & (j < ns))(functools.partial(store_histories, 0))
        cfull_ref[pl.ds(CONV_PAD - CONV_HIST, CONV_HIST), :] = (
            cfull_ref[pl.ds(tps + CONV_PAD - CONV_HIST, CONV_HIST), :])
        pfull_ref[pl.ds(POOL_PAD - POOL_HIST, POOL_HIST), :] = (
            pfull_ref[pl.ds(tps + POOL_PAD - POOL_HIST, POOL_HIST), :])
    else:
        @pl.when(j < ns)
        def _():
            for seg in range(spt):
                store_histories(seg)


def _layer(li, pos0, x, mem_k, mem_v, hg0, conv0, pool0, p):
    b, t, _ = x.shape
    tps = min(t, TOKEN_TILE)
    spt = TOKEN_TILE // tps
    assert t % tps == 0 and tps % CHUNK == 0 and b % spt == 0 and (spt == 1 or t == tps)
    tt = spt * tps
    nt = t // tps
    ns = (b // spt) * nt
    nchunks = tt // CHUNK
    kern = functools.partial(_layer_kernel, li, pos0, spt, tps, nt, ns, li == DEPTH - 1)

    front = lambda j: jnp.minimum(j, ns - 1)
    back = lambda j: jnp.maximum(j - 1, 0)

    def resident(shape):
        zeros = (0,) * len(shape)
        return pl.BlockSpec((None,) + tuple(shape), lambda j: (li,) + zeros, pipeline_mode=pl.Buffered(1))

    stream_mode = dict(pipeline_mode=pl.Buffered(1)) if spt > 1 else {}

    def per_stream(shape, tile, **mode):
        zeros = (0,) * len(shape)
        return pl.BlockSpec((spt,) + tuple(shape), lambda j: (tile(j) // nt,) + zeros, **mode)

    def tile_spec(tile):
        return pl.BlockSpec((spt, tps, D_MODEL), lambda j: (tile(j) // nt, tile(j) % nt, 0))

    kv_spec = pl.BlockSpec((None, spt, N_MEM, D_MODEL), lambda j: (li, back(j) // nt, 0, 0), **stream_mode)
    seg_c, seg_p = CONV_PAD + tps, POOL_PAD + tps
    return pl.pallas_call(
        kern,
        grid=(ns + 1,),
        in_specs=[
            tile_spec(front),
            resident((1, D_MODEL)),
            resident((D_MODEL, D_IN)),
            pl.BlockSpec((DEPTH, HG_WIDTH), lambda j: (0, 0)),
            resident((1, HG_WIDTH)),
            resident((CONV_K, CONV_WIDTH)),
            resident((1, CONV_WIDTH)),
            resident((1, CONV_WIDTH)),
            resident((1, CONV_WIDTH)),
            resident((POOL_WIDTH, POOL_WIDTH)),
            resident((1, POOL_WIDTH)),
            resident((D_MODEL, D_MODEL)),
            per_stream((HG_HEADS, HG_DK, HG_DK), back, **stream_mode),
            per_stream((CONV_HIST, CONV_WIDTH), front),
            per_stream((POOL_HIST, POOL_WIDTH), front),
            resident((1, D_MODEL)),
            resident((D_MODEL, D_MODEL)),
            resident((D_MODEL, D_MODEL)),
            kv_spec, kv_spec,
            resident((1, D_MODEL)),
            resident((D_MODEL, D_FF)),
            resident((D_FF, D_MODEL)),
            pl.BlockSpec((1, D_MODEL), lambda j: (0, 0)),
        ],
        out_specs=[
            tile_spec(back),
            per_stream((HG_HEADS, HG_DK, HG_DK), back),
            per_stream((CONV_HIST, CONV_WIDTH), front),
            per_stream((POOL_HIST, POOL_WIDTH), front),
        ],
        out_shape=[
            jax.ShapeDtypeStruct(x.shape, F32),
            jax.ShapeDtypeStruct((b, HG_HEADS, HG_DK, HG_DK), F32),
            jax.ShapeDtypeStruct((b, CONV_HIST, CONV_WIDTH), F32),
            jax.ShapeDtypeStruct((b, POOL_HIST, POOL_WIDTH), F32),
        ],
        scratch_shapes=[
            pltpu.VMEM((tt, D_IN), F32),
            pltpu.VMEM((tt, HG_WIDTH), F32),
            pltpu.VMEM((tt, D_MODEL), BF16),
            pltpu.VMEM((tt, POOL_WIDTH), BF16),
            pltpu.VMEM((HG_HEADS, HG_DK, HG_DK), F32),
            pltpu.VMEM((spt * seg_c, CONV_WIDTH), F32),
            pltpu.VMEM((SUBLANES, spt * seg_c, CONV_WIDTH), F32),
            pltpu.VMEM((spt * seg_p, POOL_WIDTH), F32),
            pltpu.VMEM((spt * seg_p, POOL_WIDTH), F32),
            pltpu.VMEM((spt * seg_p, POOL_WIDTH), F32),
            pltpu.VMEM((spt * seg_p, 128), F32),
            pltpu.VMEM((HG_HEADS, tt, CHUNK), BF16),
            pltpu.VMEM((tt, HG_WIDTH), BF16),
            pltpu.VMEM((tt, HG_WIDTH), BF16),
            pltpu.VMEM((tt, HG_WIDTH), BF16),
            pltpu.VMEM((tt, HG_WIDTH), F32),
            pltpu.VMEM((nchunks, 1, HG_WIDTH), F32),
            pltpu.VMEM((tt, D_MODEL), F32),
        ],
        compiler_params=pltpu.CompilerParams(
            dimension_semantics=("arbitrary",), vmem_limit_bytes=VMEM_LIMIT_V7X),
        name=f"layer{li}",
    )(x, p["norm_mix_g"], p["w_in"], p["lb_param"], p["hg_norm_g"], p["conv_w"], p["conv_b"],
      p["conv_ln_g"], p["conv_ln_b"], p["pool_w_bd"], p["pool_scale"], p["w_out"], hg0, conv0, pool0,
      p["norm_x_g"], p["xq_w"], p["xo_w"], mem_k, mem_v,
      p["norm_ffn_g"], p["w_up"], p["w_down"], p["final_g"])


def _memkv_kernel(m_ref, g_ref, wk_ref, wv_ref, k_ref, v_ref, kb_ref, vb_ref):
    h = _rms(m_ref[...], g_ref[...]).astype(BF16)
    k = _dot(h, wk_ref[...])
    v = _dot(h, wv_ref[...])
    k_ref[...] = k
    v_ref[...] = v
    kb_ref[...] = k.astype(BF16)
    vb_ref[...] = v.astype(BF16)


def _memkv(mem, p):
    b = mem.shape[0]
    n = b * N_MEM
    rt = 1024
    out_spec = pl.BlockSpec((None, rt, D_MODEL), lambda l, r: (l, r, 0))
    w_spec = pl.BlockSpec((None, D_MODEL, D_MODEL), lambda l, r: (l, 0, 0))
    return pl.pallas_call(
        _memkv_kernel,
        grid=(DEPTH, n // rt),
        in_specs=[
            pl.BlockSpec((rt, D_MODEL), lambda l, r: (r, 0)),
            pl.BlockSpec((None, 1, D_MODEL), lambda l, r: (l, 0, 0)),
            w_spec, w_spec,
        ],
        out_specs=[out_spec] * 4,
        out_shape=[jax.ShapeDtypeStruct((DEPTH, n, D_MODEL), F32)] * 2
        + [jax.ShapeDtypeStruct((DEPTH, n, D_MODEL), BF16)] * 2,
        compiler_params=pltpu.CompilerParams(
            dimension_semantics=("arbitrary", "arbitrary"), vmem_limit_bytes=VMEM_LIMIT_V7X),
        name="memkv",
    )(mem.reshape(n, D_MODEL), p["norm_mem_g"], p["xk_w"], p["xv_w"])


def _trunk(x, mem_k, mem_v, hg_state, conv_buf, pool_buf, pos0, p):
    new_hg, new_conv, new_pool = [], [], []
    for li in range(DEPTH):
        x, s_hg, s_conv, s_pool = _layer(li, pos0, x, mem_k, mem_v, hg_state[li], conv_buf[li], pool_buf[li], p)
        new_hg.append(s_hg)
        new_conv.append(s_conv)
        new_pool.append(s_pool)
    return x, jnp.stack(new_hg), jnp.stack(new_conv), jnp.stack(new_pool)


def kernel(x_prompt, x_sample, mem_prompt, state_hgrn, cache_conv, cache_pool, cache_mem_k, cache_mem_v,
           norm_mix_g, w_in, lb_param, hg_norm_g, conv_w, conv_b, conv_ln_g, conv_ln_b, pool_w, pool_scale,
           w_out, norm_x_g, norm_mem_g, xq_w, xk_w, xv_w, xo_w, norm_ffn_g, w_up, w_down, final_g):
    row = lambda a: a.reshape(a.shape[0], 1, a.shape[1])
    groups = pool_w.shape[1]
    pool_w_bd = (pool_w[:, :, :, None, :] * jnp.eye(groups, dtype=pool_w.dtype)[None, :, None, :, None]
                 ).reshape(DEPTH, POOL_WIDTH, POOL_WIDTH)
    p = dict(
        norm_mix_g=row(norm_mix_g), w_in=w_in.astype(BF16), lb_param=lb_param, hg_norm_g=row(hg_norm_g),
        conv_w=conv_w, conv_b=row(conv_b), conv_ln_g=row(conv_ln_g), conv_ln_b=row(conv_ln_b),
        pool_w_bd=pool_w_bd.astype(BF16), pool_scale=row(pool_scale), w_out=w_out.astype(BF16),
        norm_x_g=row(norm_x_g), norm_mem_g=row(norm_mem_g), xq_w=xq_w.astype(BF16), xk_w=xk_w.astype(BF16),
        xv_w=xv_w.astype(BF16), xo_w=xo_w.astype(BF16), norm_ffn_g=row(norm_ffn_g),
        w_up=w_up.astype(BF16), w_down=w_down.astype(BF16),
        final_g=final_g.reshape(1, D_MODEL))

    bp = x_prompt.shape[0]
    mem_k_p, mem_v_p, mem_kb, mem_vb = _memkv(mem_prompt, p)
    kv_shape = (DEPTH, bp, N_MEM, X_HEADS, X_HDIM)
    flat_kv = (DEPTH, bp, N_MEM, D_MODEL)

    dt = x_prompt.dtype
    hg0 = jnp.zeros((DEPTH, bp, HG_HEADS, HG_DK, HG_DK), dt)
    conv0 = jnp.zeros((DEPTH, bp, CONV_HIST, CONV_WIDTH), dt)
    pool0 = jnp.zeros((DEPTH, bp, POOL_HIST, POOL_WIDTH), dt)
    y_prompt, hg_p, conv_p, pool_p = _trunk(
        x_prompt, mem_kb.reshape(flat_kv), mem_vb.reshape(flat_kv), hg0, conv0, pool0, 0, p)

    bs = x_sample.shape[0]
    y_sample, hg_s, conv_s, pool_s = _trunk(
        x_sample, cache_mem_k.astype(BF16).reshape(DEPTH, bs, N_MEM, D_MODEL),
        cache_mem_v.astype(BF16).reshape(DEPTH, bs, N_MEM, D_MODEL),
        state_hgrn, cache_conv, cache_pool, PAST_LEN, p)

    return (y_prompt, y_sample, hg_p, conv_p, pool_p, mem_k_p.reshape(kv_shape), mem_v_p.reshape(kv_shape),
            hg_s, conv_s, pool_s)
```
